```python
import math
import jax, jax.numpy as jnp
from jax import lax
import numpy as np

D_MODEL = 2048
BATCH = 2
SEQ = 4096
DEPTH = 1

GLA_HEADS = 4
GLA_DK = D_MODEL // 2 // GLA_HEADS
GLA_DV = D_MODEL // GLA_HEADS
GLA_KW = GLA_HEADS * GLA_DK
GLA_VW = GLA_HEADS * GLA_DV
GLA_GATE_RANK = 16
GLA_GATE_NORMALIZER = 16.0
GLA_CHUNK = 64
DIFF_HD = 64
DIFF_HEADS = D_MODEL // (2 * DIFF_HD)
DIFF_VD = 2 * DIFF_HD
DIFF_QW = DIFF_HEADS * 2 * DIFF_HD
DIFF_VW = DIFF_HEADS * DIFF_VD
ROPE_THETA = 10000.0
Q_BLOCK = 128
N_EXPERTS = 16
D_FF = 5632
CAPACITY_FACTOR = 2
NORM_EPS = 1e-6
SUBLN_EPS = 1e-5

IN_SPLITS = (GLA_KW, GLA_KW, GLA_VW, GLA_VW, GLA_GATE_RANK, GLA_GATE_RANK,
             DIFF_QW, DIFF_QW, DIFF_VW, D_MODEL, D_MODEL)
IN_WIDTH = sum(IN_SPLITS)

kernel_name = "hybrid_gla_diffattn_ecmoe_encoder"


def rms_norm(x, w, eps=NORM_EPS):
    xf = x.astype(jnp.float32)
    y = xf * lax.rsqrt(jnp.mean(xf * xf, axis=-1, keepdims=True) + eps)
    return (y * w.astype(jnp.float32)).astype(x.dtype)


def apply_rope(x, cos, sin):
    half = x.shape[-1] // 2
    xf = x.astype(jnp.float32)
    x1, x2 = xf[..., :half], xf[..., half:]
    c = cos[None, :, None, :]
    s = sin[None, :, None, :]
    return jnp.concatenate([x1 * c - x2 * s, x2 * c + x1 * s], axis=-1).astype(x.dtype)


def gla_chunked(q, k, v, g, include_diag):
    B, H, T, dk = q.shape
    dv = v.shape[-1]
    n = T // GLA_CHUNK
    q = q.reshape(B, H, n, GLA_CHUNK, dk)
    k = k.reshape(B, H, n, GLA_CHUNK, dk)
    v = v.reshape(B, H, n, GLA_CHUNK, dv)
    b = jnp.cumsum(g.reshape(B, H, n, GLA_CHUNK, dk), axis=3)
    b_mid = b[:, :, :, GLA_CHUNK // 2 - 1:GLA_CHUNK // 2]
    b_last = b[:, :, :, -1:]
    scores = jnp.einsum('bhnid,bhnjd->bhnij', q * jnp.exp(b - b_mid), k * jnp.exp(b_mid - b))
    mask = jnp.tril(jnp.ones((GLA_CHUNK, GLA_CHUNK), dtype=bool), k=0 if include_diag else -1)
    o_intra = jnp.einsum('bhnij,bhnje->bhnie', jnp.where(mask, scores, 0.0), v)
    q_start = q * jnp.exp(b)
    k_end = k * jnp.exp(b_last - b)
    chunk_decay = jnp.exp(b_last[:, :, :, 0])

    def step(state, inp):
        qs, ke, vc, dec = inp
        o = jnp.einsum('bhld,bhde->bhle', qs, state)
        state = dec[..., None] * state + jnp.einsum('bhld,bhle->bhde', ke, vc)
        return state, o

    xs = (jnp.moveaxis(q_start, 2, 0), jnp.moveaxis(k_end, 2, 0),
          jnp.moveaxis(v, 2, 0), jnp.moveaxis(chunk_decay, 2, 0))
    state0 = jnp.zeros((B, H, dk, dv), q.dtype)
    _, o_inter = lax.scan(step, state0, xs)
    o = o_intra + jnp.moveaxis(o_inter, 0, 2)
    return o.reshape(B, H, T, dv)


def gla_branch(q, k, v, r, z_f, z_b, up_f, bias_f, up_b, bias_b, norm_w):
    B, T, _ = q.shape

    def heads(a, d):
        return a.astype(jnp.float32).reshape(B, T, GLA_HEADS, d).transpose(0, 2, 1, 3)

    qh = heads(q, GLA_DK) * (GLA_DK ** -0.5)
    kh = heads(k, GLA_DK)
    vh = heads(v, GLA_DV)
    g_f = jax.nn.log_sigmoid((z_f @ up_f + bias_f).astype(jnp.float32)) / GLA_GATE_NORMALIZER
    g_b = jax.nn.log_sigmoid((z_b @ up_b + bias_b).astype(jnp.float32)) / GLA_GATE_NORMALIZER
    g_f = heads(g_f, GLA_DK)
    g_b = heads(g_b, GLA_DK)
    flip = lambda a: jnp.flip(a, axis=2)
    o_f = gla_chunked(qh, kh, vh, g_f, True)
    o_b = flip(gla_chunked(flip(qh), flip(kh), flip(vh), flip(g_b), False))
    o = (o_f + o_b).transpose(0, 2, 1, 3)
    o = rms_norm(o, norm_w, NORM_EPS).reshape(B, T, GLA_VW)
    return (o * jax.nn.silu(r.astype(jnp.float32))).astype(r.dtype)


def diff_attention(q, k, v, lam, subln_w, lam_init):
    B, T = q.shape[0], q.shape[1]
    q = q.reshape(B, T, DIFF_HEADS, 2, DIFF_HD) * (DIFF_HD ** -0.5)
    k = k.reshape(B, T, DIFF_HEADS, 2, DIFF_HD)
    nb = T // Q_BLOCK
    qb = q.reshape(B, nb, Q_BLOCK, DIFF_HEADS, 2, DIFF_HD).transpose(1, 0, 2, 3, 4, 5)

    def block(qi):
        s = jnp.einsum('bqhcd,bkhcd->bhcqk', qi, k).astype(jnp.float32)
        p = jax.nn.softmax(s, axis=-1)
        a = p[:, :, 0] - lam * p[:, :, 1]
        return jnp.einsum('bhqk,bkhe->bqhe', a.astype(v.dtype), v)

    o = lax.map(block, qb)
    o = o.transpose(1, 0, 2, 3, 4).reshape(B, T, DIFF_HEADS, DIFF_VD)
    o = rms_norm(o, subln_w, SUBLN_EPS) * (1.0 - lam_init)
    return o.reshape(B, T, DIFF_VW)


def expert_choice_ffn(h, w_router, w_gate, w_up, w_down):
    B, T, D = h.shape
    cap = CAPACITY_FACTOR * T // N_EXPERTS
    aff = jax.nn.softmax((h @ w_router).astype(jnp.float32), axis=-1)
    G, I = lax.top_k(aff.transpose(0, 2, 1), cap)
    bidx = jnp.arange(B)[:, None, None]
    xe = h[bidx, I]
    hid = jax.nn.silu(jnp.einsum('becd,edf->becf', xe, w_gate)) * jnp.einsum('becd,edf->becf', xe, w_up)
    ye = jnp.einsum('becf,efd->becd', hid, w_down) * G[..., None].astype(h.dtype)
    return jnp.zeros_like(h).at[bidx, I].add(ye)


def setup_inputs(seed: int = 0) -> dict:
    key = jax.random.key(seed)
    ks = jax.random.split(key, 24)
    f32 = jnp.float32
    nrm = lambda k, shape, scale: jax.random.normal(k, shape, f32) * scale
    gain = lambda k, shape: 1.0 + 0.02 * jax.random.normal(k, shape, f32)
    return {
        "x": nrm(ks[0], (BATCH, SEQ, D_MODEL), 1.0),
        "norm_mix_w": gain(ks[1], (DEPTH, D_MODEL)),
        "w_in": nrm(ks[2], (DEPTH, D_MODEL, IN_WIDTH), D_MODEL ** -0.5),
        "gla_gate_up_fwd": nrm(ks[3], (DEPTH, GLA_GATE_RANK, GLA_KW), GLA_GATE_RANK ** -0.5),
        "gla_gate_bias_fwd": nrm(ks[4], (DEPTH, GLA_KW), 0.1),
        "gla_gate_up_bwd": nrm(ks[5], (DEPTH, GLA_GATE_RANK, GLA_KW), GLA_GATE_RANK ** -0.5),
        "gla_gate_bias_bwd": nrm(ks[6], (DEPTH, GLA_KW), 0.1),
        "gla_norm_w": gain(ks[7], (DEPTH, GLA_DV)),
        "diff_lambda_q1": nrm(ks[8], (DEPTH, DIFF_HD), 0.1),
        "diff_lambda_k1": nrm(ks[9], (DEPTH, DIFF_HD), 0.1),
        "diff_lambda_q2": nrm(ks[10], (DEPTH, DIFF_HD), 0.1),
        "diff_lambda_k2": nrm(ks[11], (DEPTH, DIFF_HD), 0.1),
        "diff_subln_w": gain(ks[12], (DEPTH, DIFF_VD)),
        "w_out": nrm(ks[13], (DEPTH, D_MODEL, D_MODEL), D_MODEL ** -0.5),
        "norm_ffn_w": gain(ks[14], (DEPTH, D_MODEL)),
        "w_router": nrm(ks[15], (DEPTH, D_MODEL, N_EXPERTS), D_MODEL ** -0.5),
        "w_gate_e": nrm(ks[16], (DEPTH, N_EXPERTS, D_MODEL, D_FF), D_MODEL ** -0.5),
        "w_up_e": nrm(ks[17], (DEPTH, N_EXPERTS, D_MODEL, D_FF), D_MODEL ** -0.5),
        "w_down_e": nrm(ks[18], (DEPTH, N_EXPERTS, D_FF, D_MODEL), D_FF ** -0.5),
        "norm_final_w": gain(ks[19], (D_MODEL,)),
    }


def reference(x, norm_mix_w, w_in, gla_gate_up_fwd, gla_gate_bias_fwd, gla_gate_up_bwd,
              gla_gate_bias_bwd, gla_norm_w, diff_lambda_q1, diff_lambda_k1, diff_lambda_q2,
              diff_lambda_k2, diff_subln_w, w_out, norm_ffn_w, w_router, w_gate_e, w_up_e,
              w_down_e, norm_final_w):
    B, T, D = x.shape
    pos = jnp.arange(T, dtype=jnp.float32)
    inv_freq = ROPE_THETA ** (-jnp.arange(0, DIFF_HD, 2, dtype=jnp.float32) / DIFF_HD)
    ang = pos[:, None] * inv_freq[None, :]
    cos, sin = jnp.cos(ang), jnp.sin(ang)
    split_points = np.cumsum(IN_SPLITS)[:-1].tolist()

    for l in range(DEPTH):
        h = rms_norm(x, norm_mix_w[l])
        proj = h @ w_in[l]
        (gq, gk, gv, gr, zf, zb, dq, dk, dv, gate_a, gate_b) = jnp.split(proj, split_points, axis=-1)

        y_a = gla_branch(gq, gk, gv, gr, zf, zb,
                         gla_gate_up_fwd[l], gla_gate_bias_fwd[l],
                         gla_gate_up_bwd[l], gla_gate_bias_bwd[l], gla_norm_w[l])

        lam_init = 0.8 - 0.6 * math.exp(-0.3 * l)
        lam = (jnp.exp(jnp.sum(diff_lambda_q1[l].astype(jnp.float32) * diff_lambda_k1[l].astype(jnp.float32)))
               - jnp.exp(jnp.sum(diff_lambda_q2[l].astype(jnp.float32) * diff_lambda_k2[l].astype(jnp.float32)))
               + lam_init)
        q_d = apply_rope(dq.reshape(B, T, 2 * DIFF_HEADS, DIFF_HD), cos, sin)
        k_d = apply_rope(dk.reshape(B, T, 2 * DIFF_HEADS, DIFF_HD), cos, sin)
        v_d = dv.reshape(B, T, DIFF_HEADS, DIFF_VD)
        y_b = diff_attention(q_d, k_d, v_d, lam, diff_subln_w[l], lam_init)

        merged = jax.nn.sigmoid(gate_a) * y_a + jax.nn.sigmoid(gate_b) * y_b
        x = x + merged @ w_out[l]

        h2 = rms_norm(x, norm_ffn_w[l])
        x = x + expert_choice_ffn(h2, w_router[l], w_gate_e[l], w_up_e[l], w_down_e[l])

    return rms_norm(x, norm_final_w)
```

```python
import functools
import math

import jax
import jax.numpy as jnp
from jax import lax
from jax.experimental import pallas as pl
from jax.experimental.pallas import tpu as pltpu

F32 = jnp.float32
BF16 = jnp.bfloat16
I32 = jnp.int32

GLA_HEADS = 4
GLA_GATE_RANK = 16
GLA_GATE_NORMALIZER = 16.0
GLA_CHUNK = 64
DIFF_HD = 64
ROPE_THETA = 10000.0
N_EXPERTS = 16
CAPACITY_FACTOR = 2
NORM_EPS = 1e-6
SUBLN_EPS = 1e-5

LANES = 128
VMEM_PHYSICAL = 64 * 1024 * 1024


def _params(semantics, vmem_mb):
    return pltpu.CompilerParams(dimension_semantics=semantics,
                                vmem_limit_bytes=vmem_mb * 1024 * 1024)


def _dot(a, b):
    return jnp.dot(a, b, preferred_element_type=F32)


def _dot_nt(a, b):
    return lax.dot_general(a, b, (((1,), (1,)), ((), ())), preferred_element_type=F32)


def _dot_tn(a, b):
    return lax.dot_general(a, b, (((0,), (0,)), ((), ())), preferred_element_type=F32)


def _split3(x):
    hi = x.astype(BF16)
    r1 = x - hi.astype(F32)
    mid = r1.astype(BF16)
    lo = (r1 - mid.astype(F32)).astype(BF16)
    return hi, mid, lo


def _rmsnorm_kernel(x_ref, w_ref, o_ref, *, eps):
    x = x_ref[...]
    ms = jnp.mean(x * x, axis=-1, keepdims=True)
    o_ref[...] = (x * lax.rsqrt(ms + eps) * w_ref[...]).astype(o_ref.dtype)


def _rmsnorm(x, w, eps, out_dtype, tm=512):
    n, d = x.shape
    return pl.pallas_call(
        functools.partial(_rmsnorm_kernel, eps=eps),
        grid=(n // tm,),
        in_specs=[pl.BlockSpec((tm, d), lambda i: (i, 0)),
                  pl.BlockSpec((1, d), lambda i: (0, 0))],
        out_specs=pl.BlockSpec((tm, d), lambda i: (i, 0)),
        out_shape=jax.ShapeDtypeStruct((n, d), out_dtype),
        compiler_params=_params(("parallel",), 32),
        name="rmsnorm",
    )(x, w.reshape(1, d))


def _matmul_kernel(h_ref, w_ref, o_ref):
    o_ref[...] = _dot(h_ref[...], w_ref[...]).astype(o_ref.dtype)


def _matmul_rope_kernel(h_ref, w_ref, cos_ref, sin_ref, o_ref, *, n_q_tiles, q_scale):
    y = _dot(h_ref[...], w_ref[...])
    tm, tn = y.shape
    scale = jnp.where(pl.program_id(1) < n_q_tiles, q_scale, 1.0).astype(F32)
    cos = cos_ref[...]
    sin = sin_ref[...]
    lane = lax.broadcasted_iota(I32, (tm, LANES), 1)
    first_half = (lane & (DIFF_HD - 1)) < (DIFF_HD // 2)
    for c in range(tn // LANES):
        yc = y[:, c * LANES:(c + 1) * LANES]
        sw = jnp.where(first_half,
                       pltpu.roll(yc, LANES - DIFF_HD // 2, 1),
                       pltpu.roll(yc, DIFF_HD // 2, 1))
        o_ref[:, c * LANES:(c + 1) * LANES] = ((yc * cos + sw * sin) * scale).astype(o_ref.dtype)


def _matmul(h, w, out_dtype, tm=1024, tn=1024, rope=None):
    n, k = h.shape
    m = w.shape[1]
    tn = min(tn, m)
    in_specs = [pl.BlockSpec((tm, k), lambda i, j: (i, 0)),
                pl.BlockSpec((k, tn), lambda i, j: (0, j))]
    args = [h, w]
    if rope is None:
        body = _matmul_kernel
    else:
        cos, sin, n_q_cols, q_scale, seq = rope
        blocks_per_seq = seq // tm
        in_specs += [pl.BlockSpec((tm, LANES), lambda i, j: (i % blocks_per_seq, 0)),
                     pl.BlockSpec((tm, LANES), lambda i, j: (i % blocks_per_seq, 0))]
        args += [cos, sin]
        body = functools.partial(_matmul_rope_kernel, n_q_tiles=n_q_cols // tn, q_scale=q_scale)
    return pl.pallas_call(
        body,
        grid=(n // tm, m // tn),
        in_specs=in_specs,
        out_specs=pl.BlockSpec((tm, tn), lambda i, j: (i, j)),
        out_shape=jax.ShapeDtypeStruct((n, m), out_dtype),
        compiler_params=_params(("parallel", "parallel"), 48),
        name="proj_rope" if rope is not None else "proj",
    )(*args)


GLA_BLOCK = 256


def _log_sigmoid(x):
    return jnp.minimum(x, 0.0) - jnp.log1p(jnp.exp(-jnp.abs(x)))


def _gla_chunk(q, k, v, g, tri, mask, mid, last, st_ref):
    g_hi, g_mid, g_lo = _split3(g)
    b = _dot(tri, g_hi) + _dot(tri, g_mid) + _dot(tri, g_lo)
    b_mid = b[mid:mid + 1, :]
    b_last = b[last:last + 1, :]
    qd = (q * jnp.exp(b - b_mid)).astype(BF16)
    kd = (k * jnp.exp(b_mid - b)).astype(BF16)
    sc = jnp.where(mask, _dot_nt(qd, kd), 0.0).astype(BF16)
    vb = v.astype(BF16)
    o = _dot(sc, vb)
    st = st_ref[...]
    o = o + _dot_nt((q * jnp.exp(b)).astype(BF16), st.astype(BF16))
    ke = (k * jnp.exp(b_last - b)).astype(BF16)
    st_ref[...] = st * jnp.exp(b_last) + _dot_tn(vb, ke)
    return o


def _gla_kernel(qf_ref, kf_ref, vf_ref, zf_ref, qb_ref, kb_ref, vb_ref, zb_ref,
                upf_ref, biasf_ref, upb_ref, biasb_ref, of_ref, ob_ref, sf_ref, sb_ref,
                *, q_scale):
    @pl.when(pl.program_id(2) == 0)
    def _():
        sf_ref[...] = jnp.zeros_like(sf_ref)
        sb_ref[...] = jnp.zeros_like(sb_ref)

    L = GLA_CHUNK
    row = lax.broadcasted_iota(I32, (L, L), 0)
    col = lax.broadcasted_iota(I32, (L, L), 1)
    tri_f = jnp.where(col <= row, 1.0, 0.0).astype(BF16)
    tri_b = jnp.where(col >= row, 1.0, 0.0).astype(BF16)
    mask_f = col <= row
    mask_b = col > row

    def gate(z_ref, up_ref, bias_ref):
        pre = _dot(z_ref[...].astype(BF16), up_ref[...]) + bias_ref[...]
        return _log_sigmoid(pre) / GLA_GATE_NORMALIZER

    g_f = gate(zf_ref, upf_ref, biasf_ref)
    g_b = gate(zb_ref, upb_ref, biasb_ref)

    n_chunks = GLA_BLOCK // L
    for c in range(n_chunks):
        r = slice(c * L, (c + 1) * L)
        of_ref[r, :] = _gla_chunk(qf_ref[r, :] * q_scale, kf_ref[r, :], vf_ref[r, :], g_f[r, :],
                                  tri_f, mask_f, L // 2 - 1, L - 1, sf_ref)
        cb = n_chunks - 1 - c
        r = slice(cb * L, (cb + 1) * L)
        ob_ref[r, :] = _gla_chunk(qb_ref[r, :] * q_scale, kb_ref[r, :], vb_ref[r, :], g_b[r, :],
                                  tri_b, mask_b, L // 2, 0, sb_ref)


def _gla(pm, z, upf, biasf, upb, biasb, batch, seq, dk, dv):
    n = pm.shape[0]
    nb = seq // GLA_BLOCK
    H = GLA_HEADS
    kw = H * dk
    fwd = lambda b, h, i: b * nb + i
    bwd = lambda b, h, i: b * nb + nb - 1 - i
    k_col = kw // dk
    v_col = 2 * kw // dv

    def specs(rowf):
        return [pl.BlockSpec((GLA_BLOCK, dk), lambda b, h, i: (rowf(b, h, i), h)),
                pl.BlockSpec((GLA_BLOCK, dk), lambda b, h, i: (rowf(b, h, i), k_col + h)),
                pl.BlockSpec((GLA_BLOCK, dv), lambda b, h, i: (rowf(b, h, i), v_col + h)),
                pl.BlockSpec((GLA_BLOCK, LANES), lambda b, h, i: (rowf(b, h, i), 0))]

    w_specs = [pl.BlockSpec((LANES, dk), lambda b, h, i: (0, h)),
               pl.BlockSpec((1, dk), lambda b, h, i: (0, h))]
    out_shape = jax.ShapeDtypeStruct((n, H * dv), F32)
    return pl.pallas_call(
        functools.partial(_gla_kernel, q_scale=dk ** -0.5),
        grid=(batch, H, nb),
        in_specs=specs(fwd) + specs(bwd) + w_specs + w_specs,
        out_specs=[pl.BlockSpec((GLA_BLOCK, dv), lambda b, h, i: (fwd(b, h, i), h)),
                   pl.BlockSpec((GLA_BLOCK, dv), lambda b, h, i: (bwd(b, h, i), h))],
        out_shape=[out_shape, out_shape],
        scratch_shapes=[pltpu.VMEM((dv, dk), F32), pltpu.VMEM((dv, dk), F32)],
        compiler_params=_params(("parallel", "parallel", "arbitrary"), 32),
        name="gla",
    )(pm, pm, pm, z, pm, pm, pm, z, upf, biasf, upb, biasb)


def _diff_attn_kernel(q_ref, k_ref, v_ref, lam_ref, w_ref, o_ref, *, lam_init, eps):
    q = q_ref[...]
    k = k_ref[...]
    v = v_ref[...]
    lane = lax.broadcasted_iota(I32, q.shape, 1)
    zero = jnp.zeros_like(q)
    outs = []
    for comp in range(2):
        keep = (lane < DIFF_HD) if comp == 0 else (lane >= DIFF_HD)
        s = _dot_nt(jnp.where(keep, q, zero), k)
        m = jnp.max(s, axis=-1, keepdims=True)
        p = jnp.exp(s - m)
        l = jnp.sum(p, axis=-1, keepdims=True)
        outs.append(_dot(p.astype(BF16), v) / l)
    lv = lam_ref[...]
    lam = (jnp.exp(jnp.sum(lv[0:1] * lv[1:2], axis=-1, keepdims=True))
           - jnp.exp(jnp.sum(lv[2:3] * lv[3:4], axis=-1, keepdims=True)) + lam_init)
    o = outs[0] - lam * outs[1]
    ms = jnp.mean(o * o, axis=-1, keepdims=True)
    o_ref[...] = (o * lax.rsqrt(ms + eps) * w_ref[...]) * (1.0 - lam_init)


def _diff_attn(qk, v, lam_vecs, subln_w, batch, seq, heads, lam_init, tq=256):
    n = qk.shape[0]
    vd = 2 * DIFF_HD
    nq = seq // tq
    return pl.pallas_call(
        functools.partial(_diff_attn_kernel, lam_init=lam_init, eps=SUBLN_EPS),
        grid=(batch, heads, nq),
        in_specs=[pl.BlockSpec((tq, vd), lambda b, h, i: (b * nq + i, h)),
                  pl.BlockSpec((seq, vd), lambda b, h, i: (b, heads + h)),
                  pl.BlockSpec((seq, vd), lambda b, h, i: (b, h)),
                  pl.BlockSpec((8, LANES), lambda b, h, i: (0, 0)),
                  pl.BlockSpec((1, vd), lambda b, h, i: (0, 0))],
        out_specs=pl.BlockSpec((tq, vd), lambda b, h, i: (b * nq + i, h)),
        out_shape=jax.ShapeDtypeStruct((n, heads * vd), F32),
        compiler_params=_params(("parallel", "parallel", "arbitrary"), 48),
        name="diff_attn",
    )(qk, qk, v, lam_vecs, subln_w.reshape(1, vd))


def _merge_kernel(of_ref, ob_ref, r_ref, yb_ref, ga_ref, gb_ref, x_ref, wout_ref, gnw_ref,
                  fnw_ref, wr_ref, x1_ref, h2_ref, aff_ref, *, dv, n_experts):
    o = of_ref[...] + ob_ref[...]
    segs = []
    for hh in range(o.shape[1] // dv):
        seg = o[:, hh * dv:(hh + 1) * dv]
        ms = jnp.mean(seg * seg, axis=-1, keepdims=True)
        segs.append(seg * lax.rsqrt(ms + NORM_EPS))
    on = jnp.concatenate(segs, axis=1) * gnw_ref[...]
    r = r_ref[...]
    ya = on * (r * jax.nn.sigmoid(r))
    merged = jax.nn.sigmoid(ga_ref[...]) * ya + jax.nn.sigmoid(gb_ref[...]) * yb_ref[...]
    x1 = x_ref[...] + _dot(merged.astype(BF16), wout_ref[...])
    x1_ref[...] = x1
    ms = jnp.mean(x1 * x1, axis=-1, keepdims=True)
    h2 = (x1 * lax.rsqrt(ms + NORM_EPS) * fnw_ref[...]).astype(BF16)
    h2_ref[...] = h2
    logits = _dot(h2, wr_ref[...])
    lane = lax.broadcasted_iota(I32, logits.shape, 1)
    logits = jnp.where(lane < n_experts, logits, -jnp.inf)
    e = jnp.exp(logits - jnp.max(logits, axis=-1, keepdims=True))
    aff_ref[...] = e / jnp.sum(e, axis=-1, keepdims=True)


def _merge(o_f, o_b, pm, yb, gates, x, wout, gnw, fnw, wr, dv, tm=128):
    n, d = x.shape
    r_col = (pm.shape[1] - d) // d
    row = lambda i: (i, 0)
    const = lambda i: (0, 0)
    return pl.pallas_call(
        functools.partial(_merge_kernel, dv=dv, n_experts=N_EXPERTS),
        grid=(n // tm,),
        in_specs=[pl.BlockSpec((tm, d), row), pl.BlockSpec((tm, d), row),
                  pl.BlockSpec((tm, d), lambda i: (i, r_col)),
                  pl.BlockSpec((tm, d), row),
                  pl.BlockSpec((tm, d), lambda i: (i, 0)), pl.BlockSpec((tm, d), lambda i: (i, 1)),
                  pl.BlockSpec((tm, d), row),
                  pl.BlockSpec((d, d), const), pl.BlockSpec((1, d), const),
                  pl.BlockSpec((1, d), const), pl.BlockSpec((d, LANES), const)],
        out_specs=[pl.BlockSpec((tm, d), row), pl.BlockSpec((tm, d), row),
                   pl.BlockSpec((tm, LANES), row)],
        out_shape=[jax.ShapeDtypeStruct((n, d), F32), jax.ShapeDtypeStruct((n, d), BF16),
                   jax.ShapeDtypeStruct((n, LANES), F32)],
        compiler_params=_params(("parallel",), 56),
        name="merge_outproj",
    )(o_f, o_b, pm, yb, gates, gates, x, wout, gnw, fnw, wr)


TOPK_LANE_BLOCK = 256


def _exclusive_prefix_count(flags, strict_upper):
    e, t = flags.shape
    carry = jnp.zeros((e, 1), F32)
    parts = []
    for blk in range(t // TOPK_LANE_BLOCK):
        f = flags[:, blk * TOPK_LANE_BLOCK:(blk + 1) * TOPK_LANE_BLOCK]
        parts.append(_dot(f.astype(BF16), strict_upper) + carry)
        carry = carry + jnp.sum(f, axis=-1, keepdims=True)
    return jnp.concatenate(parts, axis=1)


def _topk_kernel(afft_ref, aff_ref, idx_ref, g_ref, posm_ref, *, cap):
    a = afft_ref[0]
    n_exp, t = a.shape
    bits = pltpu.bitcast(a, I32)

    def search(i, thr):
        cand = thr | jnp.left_shift(jnp.int32(1), 30 - i)
        cnt = jnp.sum(jnp.where(bits >= cand, 1.0, 0.0), axis=-1, keepdims=True)
        return jnp.where(cnt >= cap, cand, thr)

    thr = lax.fori_loop(0, 31, search, jnp.zeros((n_exp, 1), I32))
    gt = jnp.where(bits > thr, 1.0, 0.0)
    eq = jnp.where(bits == thr, 1.0, 0.0)
    need = cap - jnp.sum(gt, axis=-1, keepdims=True)

    r = lax.broadcasted_iota(I32, (TOPK_LANE_BLOCK, TOPK_LANE_BLOCK), 0)
    c = lax.broadcasted_iota(I32, (TOPK_LANE_BLOCK, TOPK_LANE_BLOCK), 1)
    strict_upper = jnp.where(r < c, 1.0, 0.0).astype(BF16)

    tie_rank = _exclusive_prefix_count(eq, strict_upper)
    sel = gt + eq * jnp.where(tie_rank < need, 1.0, 0.0)
    pos = _exclusive_prefix_count(sel, strict_upper)
    posm_ref[...] = jnp.where(sel > 0.5, pos, -1.0)

    av = aff_ref[0]
    hi, mid, lo = _split3(av)
    lane = lax.broadcasted_iota(I32, av.shape, 1)
    tok = lax.broadcasted_iota(I32, av.shape, 0)
    digits = jnp.where(lane == 0, (tok >> 6).astype(F32),
                       jnp.where(lane == 1, (tok & 63).astype(F32), 0.0))
    feat = (digits + pltpu.roll(hi.astype(F32), 2, 1) + pltpu.roll(mid.astype(F32), 2 + n_exp, 1)
            + pltpu.roll(lo.astype(F32), 2 + 2 * n_exp, 1)).astype(BF16)

    slot = lax.broadcasted_iota(I32, (cap, t), 0).astype(F32)
    out_lane = lax.broadcasted_iota(I32, (cap, LANES), 1)

    def per_expert(e, _):
        onehot = jnp.where(posm_ref[pl.ds(e, 1), :] == slot, 1.0, 0.0).astype(BF16)
        res = _dot(onehot, feat)
        idx_ref[0, e] = (res[:, 0:1] * 64.0 + res[:, 1:2]).astype(I32)
        mine = (out_lane == 2 + e) | (out_lane == 2 + n_exp + e) | (out_lane == 2 + 2 * n_exp + e)
        g_ref[0, e] = jnp.sum(jnp.where(mine, res, 0.0), axis=-1, keepdims=True)
        return 0

    lax.fori_loop(0, n_exp, per_expert, 0)


def _topk(afft, aff, cap):
    batch, n_exp, t = afft.shape
    return pl.pallas_call(
        functools.partial(_topk_kernel, cap=cap),
        grid=(batch,),
        in_specs=[pl.BlockSpec((1, n_exp, t), lambda b: (b, 0, 0)),
                  pl.BlockSpec((1, t, LANES), lambda b: (b, 0, 0))],
        out_specs=[pl.BlockSpec((1, n_exp, cap, 1), lambda b: (b, 0, 0, 0)),
                   pl.BlockSpec((1, n_exp, cap, 1), lambda b: (b, 0, 0, 0))],
        out_shape=[jax.ShapeDtypeStruct((batch, n_exp, cap, 1), I32),
                   jax.ShapeDtypeStruct((batch, n_exp, cap, 1), F32)],
        scratch_shapes=[pltpu.VMEM((n_exp, t), F32)],
        compiler_params=_params(("parallel",), 48),
        name="expert_topk",
    )(afft, aff)


def _dispatch_kernel(idx_ref, h_ref, o_ref, *, cap, n_exp):
    base = (pl.program_id(0) * n_exp + pl.program_id(1)) * cap

    def body(c, _):
        o_ref[0, 0, c] = h_ref[0, idx_ref[base + c]]
        return 0

    lax.fori_loop(0, cap, body, 0, unroll=8)


def _dispatch(idx_flat, h3, cap):
    batch, t, s, l = h3.shape
    return pl.pallas_call(
        functools.partial(_dispatch_kernel, cap=cap, n_exp=N_EXPERTS),
        grid_spec=pltpu.PrefetchScalarGridSpec(
            num_scalar_prefetch=1,
            grid=(batch, N_EXPERTS),
            in_specs=[pl.BlockSpec((1, t, s, l), lambda b, e, idx: (b, 0, 0, 0))],
            out_specs=pl.BlockSpec((1, 1, cap, s, l), lambda b, e, idx: (e, b, 0, 0, 0))),
        out_shape=jax.ShapeDtypeStruct((N_EXPERTS, batch, cap, s, l), h3.dtype),
        compiler_params=_params(("arbitrary", "arbitrary"), 48),
        name="dispatch",
    )(idx_flat, h3)


def _ffn_kernel(x_ref, wg_ref, wu_ref, wd_ref, g_ref, o_ref):
    f = pl.program_id(1)

    @pl.when(f == 0)
    def _():
        o_ref[...] = jnp.zeros_like(o_ref)

    x = x_ref[0]
    a = _dot(x, wg_ref[0].astype(BF16))
    u = _dot(x, wu_ref[0].astype(BF16))
    hid = (a * jax.nn.sigmoid(a) * u).astype(BF16)
    o_ref[0] += _dot(hid, wd_ref[0].astype(BF16))

    @pl.when(f == pl.num_programs(1) - 1)
    def _():
        o_ref[0] = o_ref[0] * g_ref[0]


def _ffn(xe, wg, wu, wd, g, tf=256):
    n_exp, rows, d = xe.shape
    dff = wg.shape[2]
    return pl.pallas_call(
        _ffn_kernel,
        grid=(n_exp, dff // tf),
        in_specs=[pl.BlockSpec((1, rows, d), lambda e, f: (e, 0, 0)),
                  pl.BlockSpec((1, d, tf), lambda e, f: (e, 0, f)),
                  pl.BlockSpec((1, d, tf), lambda e, f: (e, 0, f)),
                  pl.BlockSpec((1, tf, d), lambda e, f: (e, f, 0)),
                  pl.BlockSpec((1, rows, 1), lambda e, f: (e, 0, 0))],
        out_specs=pl.BlockSpec((1, rows, d), lambda e, f: (e, 0, 0)),
        out_shape=jax.ShapeDtypeStruct((n_exp, rows, d), F32),
        compiler_params=_params(("parallel", "arbitrary"), 56),
        name="expert_ffn",
    )(xe, wg, wu, wd, g)


COMBINE_ROWS = 1024


def _combine_kernel(idx_ref, x1_ref, ye_ref, w_ref, o_ref, *, cap, n_exp):
    b = pl.program_id(0)
    q = pl.program_id(1)
    e = pl.program_id(2)
    base = (b * n_exp + e) * cap
    t0 = q * COMBINE_ROWS

    @pl.when(e == 0)
    def _():
        o_ref[...] = x1_ref[...]

    def lower_bound(target):
        def step(_, lohi):
            lo, hi = lohi
            mid = (lo + hi) // 2
            below = idx_ref[base + jnp.minimum(mid, cap - 1)] < target
            active = lo < hi
            return (jnp.where(active & below, mid + 1, lo),
                    jnp.where(active & jnp.logical_not(below), mid, hi))
        lo, _ = lax.fori_loop(0, cap.bit_length() + 1, step, (jnp.int32(0), jnp.int32(cap)))
        return lo

    c0 = lower_bound(t0)
    c1 = lower_bound(t0 + COMBINE_ROWS)

    def add_row(c, _):
        t = idx_ref[base + c] - t0
        o_ref[pl.ds(t, 1), :] = o_ref[pl.ds(t, 1), :] + ye_ref[0, pl.ds(c, 1), :]
        return 0

    lax.fori_loop(c0, c1, add_row, 0)

    @pl.when(e == n_exp - 1)
    def _():
        x = o_ref[...]
        ms = jnp.mean(x * x, axis=-1, keepdims=True)
        o_ref[...] = x * lax.rsqrt(ms + NORM_EPS) * w_ref[...]


def _combine(idx_flat, x1, ye, w, batch, seq, cap):
    n, d = x1.shape
    nq = seq // COMBINE_ROWS
    return pl.pallas_call(
        functools.partial(_combine_kernel, cap=cap, n_exp=N_EXPERTS),
        grid_spec=pltpu.PrefetchScalarGridSpec(
            num_scalar_prefetch=1,
            grid=(batch, nq, N_EXPERTS),
            in_specs=[pl.BlockSpec((COMBINE_ROWS, d), lambda b, q, e, idx: (b * nq + q, 0)),
                      pl.BlockSpec((1, cap, d), lambda b, q, e, idx: (e, b, 0)),
                      pl.BlockSpec((1, d), lambda b, q, e, idx: (0, 0))],
            out_specs=pl.BlockSpec((COMBINE_ROWS, d), lambda b, q, e, idx: (b * nq + q, 0))),
        out_shape=jax.ShapeDtypeStruct((n, d), F32),
        compiler_params=_params(("arbitrary", "arbitrary", "arbitrary"), 56),
        name="combine_norm",
    )(idx_flat, x1, ye, w.reshape(1, d))


def _rope_tables(seq):
    pos = jnp.arange(seq, dtype=F32)
    inv_freq = ROPE_THETA ** (-jnp.arange(0, DIFF_HD, 2, dtype=F32) / DIFF_HD)
    ang = pos[:, None] * inv_freq[None, :]
    cos, sin = jnp.cos(ang), jnp.sin(ang)
    reps = LANES // (DIFF_HD // 2)
    sign = jnp.tile(jnp.concatenate([-jnp.ones((DIFF_HD // 2,), F32), jnp.ones((DIFF_HD // 2,), F32)]),
                    LANES // DIFF_HD)
    return jnp.tile(cos, (1, reps)), jnp.tile(sin, (1, reps)) * sign[None, :]


def kernel(x, norm_mix_w, w_in, gla_gate_up_fwd, gla_gate_bias_fwd, gla_gate_up_bwd, gla_gate_bias_bwd, gla_norm_w, diff_lambda_q1, diff_lambda_k1, diff_lambda_q2, diff_lambda_k2, diff_subln_w, w_out, norm_ffn_w, w_router, w_gate_e, w_up_e, w_down_e, norm_final_w):
    batch, seq, d = x.shape
    depth = w_in.shape[0]
    assert depth == 1, "the combine stage applies the final norm, so it must follow the only layer"
    n = batch * seq
    kw = gla_gate_up_fwd.shape[2]
    dk = kw // GLA_HEADS
    dv = d // GLA_HEADS
    rank = gla_gate_up_fwd.shape[1]
    diff_heads = d // (2 * DIFF_HD)
    cap = CAPACITY_FACTOR * seq // N_EXPERTS
    cos_t, sin_t = _rope_tables(seq)

    o_z = 2 * kw + 2 * d
    o_dq = o_z + 2 * rank
    o_dv = o_dq + 2 * d
    o_gate = o_dv + d

    xf = x.reshape(n, d)
    for l in range(depth):
        w = w_in[l]
        w_main = w[:, :o_z].astype(BF16)
        w_z = jnp.pad(w[:, o_z:o_dq], ((0, 0), (0, LANES - 2 * rank))).astype(BF16)
        w_qk = w[:, o_dq:o_dv].astype(BF16)
        w_v = w[:, o_dv:o_gate].astype(BF16)
        w_g = w[:, o_gate:].astype(BF16)

        h = _rmsnorm(xf, norm_mix_w[l], NORM_EPS, BF16)
        pm = _matmul(h, w_main, F32)
        z = _matmul(h, w_z, F32)
        qk = _matmul(h, w_qk, BF16, rope=(cos_t, sin_t, d, DIFF_HD ** -0.5, seq))
        vd = _matmul(h, w_v, BF16)
        gates = _matmul(h, w_g, F32)

        zero_rows = lambda a, before: jnp.pad(a, ((before, LANES - rank - before), (0, 0))).astype(BF16)
        o_f, o_b = _gla(pm, z, zero_rows(gla_gate_up_fwd[l], 0), gla_gate_bias_fwd[l].reshape(1, kw),
                        zero_rows(gla_gate_up_bwd[l], rank), gla_gate_bias_bwd[l].reshape(1, kw),
                        batch, seq, dk, dv)

        lam_init = 0.8 - 0.6 * math.exp(-0.3 * l)
        lam_vecs = jnp.pad(jnp.stack([diff_lambda_q1[l], diff_lambda_k1[l],
                                      diff_lambda_q2[l], diff_lambda_k2[l]]).astype(F32),
                           ((0, 4), (0, LANES - DIFF_HD)))
        yb = _diff_attn(qk, vd, lam_vecs, diff_subln_w[l], batch, seq, diff_heads, lam_init)

        wr = jnp.pad(w_router[l], ((0, 0), (0, LANES - N_EXPERTS))).astype(BF16)
        x1, h2, aff = _merge(o_f, o_b, pm, yb, gates, xf, w_out[l].astype(BF16),
                             jnp.tile(gla_norm_w[l], GLA_HEADS).reshape(1, d),
                             norm_ffn_w[l].reshape(1, d), wr, dv)

        aff3 = aff.reshape(batch, seq, LANES)
        afft = jnp.transpose(aff3[:, :, :N_EXPERTS], (0, 2, 1))
        idx, g = _topk(afft, aff3, cap)
        idx_flat = idx.reshape(batch * N_EXPERTS * cap)
        g_e = jnp.transpose(g, (1, 0, 2, 3)).reshape(N_EXPERTS, batch * cap, 1)

        xe = _dispatch(idx_flat, h2.reshape(batch, seq, d // LANES, LANES), cap)
        xe = xe.reshape(N_EXPERTS, batch * cap, d)
        ye = _ffn(xe, w_gate_e[l], w_up_e[l], w_down_e[l], g_e)

        xf = _combine(idx_flat, x1, ye, norm_final_w, batch, seq, cap)
    return xf.reshape(batch, seq, d)
```

```python
import functools
import math

import jax
import jax.numpy as jnp
from jax import lax
from jax.experimental import pallas as pl
from jax.experimental.pallas import tpu as pltpu

F32 = jnp.float32
BF16 = jnp.bfloat16
I32 = jnp.int32
U32 = jnp.uint32

GLA_HEADS = 4
GLA_GATE_RANK = 16
GLA_GATE_NORMALIZER = 16.0
GLA_CHUNK = 64
DIFF_HD = 64
ROPE_THETA = 10000.0
N_EXPERTS = 16
CAPACITY_FACTOR = 2
NORM_EPS = 1e-6
SUBLN_EPS = 1e-5

LANES = 128
VMEM_PHYSICAL = 64 * 1024 * 1024


def _params(semantics, vmem_mb):
    return pltpu.CompilerParams(dimension_semantics=semantics,
                                vmem_limit_bytes=vmem_mb * 1024 * 1024)


def _dot(a, b):
    return jnp.dot(a, b, preferred_element_type=F32)


def _dot_nt(a, b):
    return lax.dot_general(a, b, (((1,), (1,)), ((), ())), preferred_element_type=F32)


def _dot_tn(a, b):
    return lax.dot_general(a, b, (((0,), (0,)), ((), ())), preferred_element_type=F32)


def _pack_halves(x):
    n = x.shape[1] // 2
    lo = pltpu.bitcast(x[:, :n].astype(F32), U32)
    hi = pltpu.bitcast(x[:, n:].astype(F32), U32)
    return (lo >> 16) | (hi & jnp.uint32(0xFFFF0000))


def _unpack_halves(p):
    lo = pltpu.bitcast(p << 16, F32).astype(BF16)
    hi = pltpu.bitcast(p & jnp.uint32(0xFFFF0000), F32).astype(BF16)
    return lo, hi


def _split3(x):
    hi = x.astype(BF16)
    r1 = x - hi.astype(F32)
    mid = r1.astype(BF16)
    lo = (r1 - mid.astype(F32)).astype(BF16)
    return hi, mid, lo


def _rmsnorm_kernel(x_ref, w_ref, o_ref, *, eps):
    x = x_ref[...]
    ms = jnp.mean(x * x, axis=-1, keepdims=True)
    o_ref[...] = (x * lax.rsqrt(ms + eps) * w_ref[...]).astype(o_ref.dtype)


def _rmsnorm(x, w, eps, out_dtype, tm=512):
    n, d = x.shape
    return pl.pallas_call(
        functools.partial(_rmsnorm_kernel, eps=eps),
        grid=(n // tm,),
        in_specs=[pl.BlockSpec((tm, d), lambda i: (i, 0)),
                  pl.BlockSpec((1, d), lambda i: (0, 0))],
        out_specs=pl.BlockSpec((tm, d), lambda i: (i, 0)),
        out_shape=jax.ShapeDtypeStruct((n, d), out_dtype),
        compiler_params=_params(("parallel",), 32),
        name="rmsnorm",
    )(x, w.reshape(1, d))


PROJ_ROW_CHUNK = 256


def _proj_kernel(*refs, shift, rope, n_q_tiles, q_scale):
    h_ref, wa_ref = refs[0], refs[1]
    pos = 2
    wb_ref = None
    if shift:
        wb_ref = refs[pos]
        pos += 1
    if rope:
        cos_ref, sin_ref = refs[pos], refs[pos + 1]
        pos += 2
    o_ref, w_scr = refs[pos], refs[pos + 1]
    k, tn = wa_ref.shape

    @pl.when(pl.program_id(1) == 0)
    def _():
        for r in range(0, k, PROJ_ROW_CHUNK):
            rows = slice(r, r + PROJ_ROW_CHUNK)
            if shift:
                wide = jnp.concatenate([wa_ref[rows, :], wb_ref[rows, :]], axis=1)
                w_scr[rows, :] = pltpu.roll(wide, tn + LANES - shift, 1)[:, :tn].astype(BF16)
            else:
                w_scr[rows, :] = wa_ref[rows, :].astype(BF16)

    y = _dot(h_ref[...], w_scr[...])
    if not rope:
        o_ref[...] = y.astype(o_ref.dtype)
        return
    tm = y.shape[0]
    scale = jnp.where(pl.program_id(0) < n_q_tiles, q_scale, 1.0).astype(F32)
    cos = cos_ref[...]
    sin = sin_ref[...]
    lane = lax.broadcasted_iota(I32, (tm, LANES), 1)
    first_half = (lane & (DIFF_HD - 1)) < (DIFF_HD // 2)
    for c in range(tn // LANES):
        yc = y[:, c * LANES:(c + 1) * LANES]
        sw = jnp.where(first_half,
                       pltpu.roll(yc, LANES - DIFF_HD // 2, 1),
                       pltpu.roll(yc, DIFF_HD // 2, 1))
        o_ref[:, c * LANES:(c + 1) * LANES] = ((yc * cos + sw * sin) * scale).astype(o_ref.dtype)


def _proj(h, w, col0, width, out_dtype, tm=1024, tn=1024, rope=None):
    n, k = h.shape
    tn = min(tn, width)
    shift = col0 % LANES
    base = col0 - shift
    assert base % tn == 0 and width % tn == 0
    in_specs = [pl.BlockSpec((tm, k), lambda j, i: (i, 0)),
                pl.BlockSpec((k, tn), lambda j, i: (0, base // tn + j))]
    args = [h, w]
    if shift:
        in_specs.append(pl.BlockSpec((k, LANES), lambda j, i: (0, (base + (j + 1) * tn) // LANES)))
        args.append(w)
    n_q_tiles, q_scale = 0, 1.0
    if rope is not None:
        cos, sin, n_q_cols, q_scale, seq = rope
        n_q_tiles = n_q_cols // tn
        blocks_per_seq = seq // tm
        in_specs += [pl.BlockSpec((tm, LANES), lambda j, i: (i % blocks_per_seq, 0)),
                     pl.BlockSpec((tm, LANES), lambda j, i: (i % blocks_per_seq, 0))]
        args += [cos, sin]
    return pl.pallas_call(
        functools.partial(_proj_kernel, shift=shift, rope=rope is not None,
                          n_q_tiles=n_q_tiles, q_scale=q_scale),
        grid=(width // tn, n // tm),
        in_specs=in_specs,
        out_specs=pl.BlockSpec((tm, tn), lambda j, i: (i, j)),
        out_shape=jax.ShapeDtypeStruct((n, width), out_dtype),
        scratch_shapes=[pltpu.VMEM((k, tn), BF16)],
        compiler_params=_params(("parallel", "arbitrary"), 52),
        name="proj_rope" if rope is not None else "proj",
    )(*args)


GLA_BLOCK = 256


def _log_sigmoid(x):
    return jnp.minimum(x, 0.0) - jnp.log1p(jnp.exp(-jnp.abs(x)))


def _gla_chunk(q, k, v, g, tri, mask, mid, last, st_ref):
    g_hi, g_mid, g_lo = _split3(g)
    b = _dot(tri, g_hi) + _dot(tri, g_mid) + _dot(tri, g_lo)
    b_mid = b[mid:mid + 1, :]
    b_last = b[last:last + 1, :]
    qd = (q * jnp.exp(b - b_mid)).astype(BF16)
    kd = (k * jnp.exp(b_mid - b)).astype(BF16)
    sc = jnp.where(mask, _dot_nt(qd, kd), 0.0).astype(BF16)
    vb = v.astype(BF16)
    o = _dot(sc, vb)
    st = st_ref[...]
    o = o + _dot_nt((q * jnp.exp(b)).astype(BF16), st.astype(BF16))
    ke = (k * jnp.exp(b_last - b)).astype(BF16)
    st_ref[...] = st * jnp.exp(b_last) + _dot_tn(vb, ke)
    return o


def _gla_kernel(qf_ref, kf_ref, vf_ref, zf_ref, qb_ref, kb_ref, vb_ref, zb_ref,
                upf_ref, biasf_ref, upb_ref, biasb_ref, of_ref, ob_ref, sf_ref, sb_ref,
                *, q_scale):
    @pl.when(pl.program_id(2) == 0)
    def _():
        sf_ref[...] = jnp.zeros_like(sf_ref)
        sb_ref[...] = jnp.zeros_like(sb_ref)

    L = GLA_CHUNK
    row = lax.broadcasted_iota(I32, (L, L), 0)
    col = lax.broadcasted_iota(I32, (L, L), 1)
    tri_f = jnp.where(col <= row, 1.0, 0.0).astype(BF16)
    tri_b = jnp.where(col >= row, 1.0, 0.0).astype(BF16)
    mask_f = col <= row
    mask_b = col > row

    def gate(z_ref, up_ref, bias_ref):
        z = z_ref[...]
        z = jnp.where(lax.broadcasted_iota(I32, z.shape, 1) < 2 * GLA_GATE_RANK, z, 0.0)
        pre = _dot(z.astype(BF16), up_ref[...]) + bias_ref[...]
        return _log_sigmoid(pre) / GLA_GATE_NORMALIZER

    g_f = gate(zf_ref, upf_ref, biasf_ref)
    g_b = gate(zb_ref, upb_ref, biasb_ref)

    n_chunks = GLA_BLOCK // L
    for c in range(n_chunks):
        r = slice(c * L, (c + 1) * L)
        of_ref[r, :] = _gla_chunk(qf_ref[r, :] * q_scale, kf_ref[r, :], vf_ref[r, :], g_f[r, :],
                                  tri_f, mask_f, L // 2 - 1, L - 1, sf_ref)
        cb = n_chunks - 1 - c
        r = slice(cb * L, (cb + 1) * L)
        ob_ref[r, :] = _gla_chunk(qb_ref[r, :] * q_scale, kb_ref[r, :], vb_ref[r, :], g_b[r, :],
                                  tri_b, mask_b, L // 2, 0, sb_ref)


def _gla(pm, z, upf, biasf, upb, biasb, batch, seq, dk, dv):
    n = pm.shape[0]
    nb = seq // GLA_BLOCK
    H = GLA_HEADS
    kw = H * dk
    fwd = lambda b, h, i: b * nb + i
    bwd = lambda b, h, i: b * nb + nb - 1 - i
    k_col = kw // dk
    v_col = 2 * kw // dv

    def specs(rowf):
        return [pl.BlockSpec((GLA_BLOCK, dk), lambda b, h, i: (rowf(b, h, i), h)),
                pl.BlockSpec((GLA_BLOCK, dk), lambda b, h, i: (rowf(b, h, i), k_col + h)),
                pl.BlockSpec((GLA_BLOCK, dv), lambda b, h, i: (rowf(b, h, i), v_col + h)),
                pl.BlockSpec((GLA_BLOCK, LANES), lambda b, h, i: (rowf(b, h, i), 0))]

    w_specs = [pl.BlockSpec((LANES, dk), lambda b, h, i: (0, h)),
               pl.BlockSpec((1, dk), lambda b, h, i: (0, h))]
    out_shape = jax.ShapeDtypeStruct((n, H * dv), F32)
    return pl.pallas_call(
        functools.partial(_gla_kernel, q_scale=dk ** -0.5),
        grid=(batch, H, nb),
        in_specs=specs(fwd) + specs(bwd) + w_specs + w_specs,
        out_specs=[pl.BlockSpec((GLA_BLOCK, dv), lambda b, h, i: (fwd(b, h, i), h)),
                   pl.BlockSpec((GLA_BLOCK, dv), lambda b, h, i: (bwd(b, h, i), h))],
        out_shape=[out_shape, out_shape],
        scratch_shapes=[pltpu.VMEM((dv, dk), F32), pltpu.VMEM((dv, dk), F32)],
        compiler_params=_params(("parallel", "parallel", "arbitrary"), 32),
        name="gla",
    )(pm, pm, pm, z, pm, pm, pm, z, upf, biasf, upb, biasb)


ATTN_TQ = 256
ATTN_TK = 256
ATTN_SUB = 64
ATTN_VROWS = 2 * DIFF_HD + 16


def _diff_attn_kernel(q_ref, k_ref, vt_ref, lam_ref, w_ref, o_ref, s_ref, *, lam_init, eps):
    seq = k_ref.shape[0]
    lv = lam_ref[...]
    lam = (jnp.exp(jnp.sum(lv[0:1] * lv[1:2], axis=-1, keepdims=True))
           - jnp.exp(jnp.sum(lv[2:3] * lv[3:4], axis=-1, keepdims=True)) + lam_init)
    lane = lax.broadcasted_iota(I32, (ATTN_TQ, 2 * DIFF_HD), 1)
    w_col = w_ref[...]

    n_tiles = seq // ATTN_TQ
    n_chunks = seq // ATTN_TK
    groups = ATTN_TK // 8

    def scores(tile, c):
        q = q_ref[pl.ds(pl.multiple_of(tile * ATTN_TQ, ATTN_TQ), ATTN_TQ), :]
        keep = (lane < DIFF_HD) if c == 0 else (lane >= DIFF_HD)
        qc = jnp.where(keep, q, jnp.zeros_like(q))
        mrun = jnp.full((8, ATTN_TQ), -jnp.inf, F32)
        for j in range(n_chunks):
            st = _dot_nt(k_ref[j * ATTN_TK:(j + 1) * ATTN_TK, :], qc)
            s_ref[c, j * ATTN_TK:(j + 1) * ATTN_TK, :] = st
            mrun = jnp.maximum(mrun, jnp.max(st.reshape(groups, 8, ATTN_TQ), axis=0))
        return jnp.max(mrun, axis=0, keepdims=True)

    def weighted_values(c, m):
        acc = jnp.zeros((vt_ref.shape[0], ATTN_TQ), F32)
        for j in range(n_chunks):
            pieces = [jnp.exp2(s_ref[c, r:r + ATTN_SUB, :] - m).astype(BF16)
                      for r in range(j * ATTN_TK, (j + 1) * ATTN_TK, ATTN_SUB)]
            acc = acc + _dot(vt_ref[:, j * ATTN_TK:(j + 1) * ATTN_TK], jnp.concatenate(pieces, axis=0))
        return acc[:2 * DIFF_HD] / acc[2 * DIFF_HD:2 * DIFF_HD + 1]

    def q_tile(i, m0):
        o0 = weighted_values(0, m0)
        m1 = scores(i, 1)
        o1 = weighted_values(1, m1)
        m0_next = scores(jnp.minimum(i + 1, n_tiles - 1), 0)
        o = o0 - lam * o1
        ms = jnp.mean(o * o, axis=0, keepdims=True)
        y = (o * lax.rsqrt(ms + eps) * w_col) * (1.0 - lam_init)
        o_ref[pl.ds(pl.multiple_of(i * ATTN_TQ, ATTN_TQ), ATTN_TQ), :] = y.T
        return m0_next

    lax.fori_loop(0, n_tiles, q_tile, scores(0, 0))


def _diff_attn(qk, vt, lam_vecs, subln_w, batch, seq, heads, lam_init):
    n = qk.shape[0]
    vd = 2 * DIFF_HD
    return pl.pallas_call(
        functools.partial(_diff_attn_kernel, lam_init=lam_init, eps=SUBLN_EPS),
        grid=(batch, heads),
        in_specs=[pl.BlockSpec((seq, vd), lambda b, h: (b, h)),
                  pl.BlockSpec((seq, vd), lambda b, h: (b, heads + h)),
                  pl.BlockSpec((None, ATTN_VROWS, seq), lambda b, h: (b * heads + h, 0, 0)),
                  pl.BlockSpec((8, LANES), lambda b, h: (0, 0)),
                  pl.BlockSpec((vd, 1), lambda b, h: (0, 0))],
        out_specs=pl.BlockSpec((seq, vd), lambda b, h: (b, h)),
        out_shape=jax.ShapeDtypeStruct((n, heads * vd), F32),
        scratch_shapes=[pltpu.VMEM((2, seq, ATTN_TQ), F32)],
        compiler_params=_params(("parallel", "parallel"), 32),
        name="diff_attn",
    )(qk, qk, vt, lam_vecs, subln_w.reshape(vd, 1))


def _merge_kernel(of_ref, ob_ref, r_ref, yb_ref, ga_ref, gb_ref, x_ref, wout_ref, gnw_ref,
                  fnw_ref, wr_ref, x1_ref, h2_ref, aff_ref, *, dv, n_experts):
    o = of_ref[...] + ob_ref[...]
    segs = []
    for hh in range(o.shape[1] // dv):
        seg = o[:, hh * dv:(hh + 1) * dv]
        ms = jnp.mean(seg * seg, axis=-1, keepdims=True)
        segs.append(seg * lax.rsqrt(ms + NORM_EPS))
    on = jnp.concatenate(segs, axis=1) * gnw_ref[...]
    r = r_ref[...]
    ya = on * (r * jax.nn.sigmoid(r))
    merged = jax.nn.sigmoid(ga_ref[...]) * ya + jax.nn.sigmoid(gb_ref[...]) * yb_ref[...]
    x1 = x_ref[...] + _dot(merged.astype(BF16), wout_ref[...])
    x1_ref[...] = x1
    ms = jnp.mean(x1 * x1, axis=-1, keepdims=True)
    h2 = (x1 * lax.rsqrt(ms + NORM_EPS) * fnw_ref[...]).astype(BF16)
    h2_ref[...] = _pack_halves(h2)
    logits = _dot(h2, wr_ref[...])
    lane = lax.broadcasted_iota(I32, logits.shape, 1)
    logits = jnp.where(lane < n_experts, logits, -jnp.inf)
    e = jnp.exp(logits - jnp.max(logits, axis=-1, keepdims=True))
    aff_ref[...] = e / jnp.sum(e, axis=-1, keepdims=True)


def _merge(o_f, o_b, pm, yb, gates, x, wout, gnw, fnw, wr, dv, tm=128):
    n, d = x.shape
    r_col = (pm.shape[1] - d) // d
    row = lambda i: (i, 0)
    const = lambda i: (0, 0)
    return pl.pallas_call(
        functools.partial(_merge_kernel, dv=dv, n_experts=N_EXPERTS),
        grid=(n // tm,),
        in_specs=[pl.BlockSpec((tm, d), row), pl.BlockSpec((tm, d), row),
                  pl.BlockSpec((tm, d), lambda i: (i, r_col)),
                  pl.BlockSpec((tm, d), row),
                  pl.BlockSpec((tm, d), lambda i: (i, 0)), pl.BlockSpec((tm, d), lambda i: (i, 1)),
                  pl.BlockSpec((tm, d), row),
                  pl.BlockSpec((d, d), const), pl.BlockSpec((1, d), const),
                  pl.BlockSpec((1, d), const), pl.BlockSpec((d, LANES), const)],
        out_specs=[pl.BlockSpec((tm, d), row), pl.BlockSpec((tm, d // 2), row),
                   pl.BlockSpec((tm, LANES), row)],
        out_shape=[jax.ShapeDtypeStruct((n, d), F32), jax.ShapeDtypeStruct((n, d // 2), U32),
                   jax.ShapeDtypeStruct((n, LANES), F32)],
        compiler_params=_params(("parallel",), 56),
        name="merge_outproj",
    )(o_f, o_b, pm, yb, gates, gates, x, wout, gnw, fnw, wr)


TOPK_LANE_BLOCK = 256


def _exclusive_prefix_count(flags, strict_upper):
    e, t = flags.shape
    carry = jnp.zeros((e, 1), F32)
    parts = []
    for blk in range(t // TOPK_LANE_BLOCK):
        f = flags[:, blk * TOPK_LANE_BLOCK:(blk + 1) * TOPK_LANE_BLOCK]
        parts.append(_dot(f.astype(BF16), strict_upper) + carry)
        carry = carry + jnp.sum(f, axis=-1, keepdims=True)
    return jnp.concatenate(parts, axis=1)


def _topk_kernel(afft_ref, aff_ref, idx_ref, g_ref, posm_ref, *, cap):
    a = afft_ref[0]
    n_exp, t = a.shape
    bits = pltpu.bitcast(a, I32)

    def search(i, thr):
        cand = thr | jnp.left_shift(jnp.int32(1), 30 - i)
        cnt = jnp.sum(jnp.where(bits >= cand, 1.0, 0.0), axis=-1, keepdims=True)
        return jnp.where(cnt >= cap, cand, thr)

    thr = lax.fori_loop(0, 31, search, jnp.zeros((n_exp, 1), I32))
    gt = jnp.where(bits > thr, 1.0, 0.0)
    eq = jnp.where(bits == thr, 1.0, 0.0)
    need = cap - jnp.sum(gt, axis=-1, keepdims=True)

    r = lax.broadcasted_iota(I32, (TOPK_LANE_BLOCK, TOPK_LANE_BLOCK), 0)
    c = lax.broadcasted_iota(I32, (TOPK_LANE_BLOCK, TOPK_LANE_BLOCK), 1)
    strict_upper = jnp.where(r < c, 1.0, 0.0).astype(BF16)

    tie_rank = _exclusive_prefix_count(eq, strict_upper)
    sel = gt + eq * jnp.where(tie_rank < need, 1.0, 0.0)
    pos = _exclusive_prefix_count(sel, strict_upper)
    posm_ref[...] = jnp.where(sel > 0.5, pos, -1.0)

    av = aff_ref[0]
    hi, mid, lo = _split3(av)
    lane = lax.broadcasted_iota(I32, av.shape, 1)
    tok = lax.broadcasted_iota(I32, av.shape, 0)
    digits = jnp.where(lane == 0, (tok >> 6).astype(F32),
                       jnp.where(lane == 1, (tok & 63).astype(F32), 0.0))
    feat = (digits + pltpu.roll(hi.astype(F32), 2, 1) + pltpu.roll(mid.astype(F32), 2 + n_exp, 1)
            + pltpu.roll(lo.astype(F32), 2 + 2 * n_exp, 1)).astype(BF16)

    slot = lax.broadcasted_iota(I32, (cap, t), 0).astype(F32)
    out_lane = lax.broadcasted_iota(I32, (cap, LANES), 1)

    def per_expert(e, _):
        onehot = jnp.where(posm_ref[pl.ds(e, 1), :] == slot, 1.0, 0.0).astype(BF16)
        res = _dot(onehot, feat)
        idx_ref[0, e] = (res[:, 0:1] * 64.0 + res[:, 1:2]).astype(I32)
        mine = (out_lane == 2 + e) | (out_lane == 2 + n_exp + e) | (out_lane == 2 + 2 * n_exp + e)
        g_ref[0, e] = jnp.sum(jnp.where(mine, res, 0.0), axis=-1, keepdims=True)
        return 0

    lax.fori_loop(0, n_exp, per_expert, 0)


def _topk(afft, aff, cap):
    batch, n_exp, t = afft.shape
    return pl.pallas_call(
        functools.partial(_topk_kernel, cap=cap),
        grid=(batch,),
        in_specs=[pl.BlockSpec((1, n_exp, t), lambda b: (b, 0, 0)),
                  pl.BlockSpec((1, t, LANES), lambda b: (b, 0, 0))],
        out_specs=[pl.BlockSpec((1, n_exp, cap, 1), lambda b: (b, 0, 0, 0)),
                   pl.BlockSpec((1, n_exp, cap, 1), lambda b: (b, 0, 0, 0))],
        out_shape=[jax.ShapeDtypeStruct((batch, n_exp, cap, 1), I32),
                   jax.ShapeDtypeStruct((batch, n_exp, cap, 1), F32)],
        scratch_shapes=[pltpu.VMEM((n_exp, t), F32)],
        compiler_params=_params(("parallel",), 48),
        name="expert_topk",
    )(afft, aff)


def _dispatch_kernel(idx_ref, h_ref, o_ref, *, cap, n_exp):
    base = (pl.program_id(0) * n_exp + pl.program_id(1)) * cap

    def body(c, _):
        o_ref[pl.ds(c, 1), :] = h_ref[pl.ds(idx_ref[base + c], 1), :]
        return 0

    lax.fori_loop(0, cap, body, 0, unroll=8)


def _dispatch(idx_flat, hp, batch, seq, cap):
    width = hp.shape[1]
    return pl.pallas_call(
        functools.partial(_dispatch_kernel, cap=cap, n_exp=N_EXPERTS),
        grid_spec=pltpu.PrefetchScalarGridSpec(
            num_scalar_prefetch=1,
            grid=(batch, N_EXPERTS),
            in_specs=[pl.BlockSpec((seq, width), lambda b, e, idx: (b, 0))],
            out_specs=pl.BlockSpec((None, cap, width), lambda b, e, idx: (e, b, 0))),
        out_shape=jax.ShapeDtypeStruct((N_EXPERTS, batch * cap, width), hp.dtype),
        compiler_params=_params(("arbitrary", "arbitrary"), 48),
        name="dispatch",
    )(idx_flat, hp)


def _ffn_kernel(x_ref, wg_ref, wu_ref, wd_ref, g_ref, o_ref, xs_ref):
    f = pl.program_id(1)
    d = xs_ref.shape[1]

    @pl.when(f == 0)
    def _():
        lo, hi = _unpack_halves(x_ref[...])
        xs_ref[:, :d // 2] = lo
        xs_ref[:, d // 2:] = hi
        o_ref[...] = jnp.zeros_like(o_ref)

    x = xs_ref[...]
    a = _dot(x, wg_ref[...].astype(BF16))
    u = _dot(x, wu_ref[...].astype(BF16))
    hid = (a * jax.nn.sigmoid(a) * u).astype(BF16)
    o_ref[...] += _dot(hid, wd_ref[...].astype(BF16))

    @pl.when(f == pl.num_programs(1) - 1)
    def _():
        o_ref[...] = o_ref[...] * g_ref[...]


def _ffn(xe, wg, wu, wd, g, tf=256):
    n_exp, rows, half = xe.shape
    d = 2 * half
    dff = wg.shape[2]
    return pl.pallas_call(
        _ffn_kernel,
        grid=(n_exp, dff // tf),
        in_specs=[pl.BlockSpec((None, rows, half), lambda e, f: (e, 0, 0)),
                  pl.BlockSpec((None, d, tf), lambda e, f: (e, 0, f)),
                  pl.BlockSpec((None, d, tf), lambda e, f: (e, 0, f)),
                  pl.BlockSpec((None, tf, d), lambda e, f: (e, f, 0)),
                  pl.BlockSpec((None, rows, 1), lambda e, f: (e, 0, 0))],
        out_specs=pl.BlockSpec((None, rows, d), lambda e, f: (e, 0, 0)),
        out_shape=jax.ShapeDtypeStruct((n_exp, rows, d), F32),
        scratch_shapes=[pltpu.VMEM((rows, d), BF16)],
        compiler_params=_params(("parallel", "arbitrary"), 56),
        name="expert_ffn",
    )(xe, wg, wu, wd, g)


COMBINE_ROWS = 1024
COMBINE_UNROLL = 4


def _combine_kernel(idx_ref, x1_ref, ye_ref, w_ref, o_ref, *, cap, n_exp):
    b = pl.program_id(0)
    q = pl.program_id(1)
    e = pl.program_id(2)
    base = (b * n_exp + e) * cap
    t0 = q * COMBINE_ROWS

    @pl.when(e == 0)
    def _():
        o_ref[...] = x1_ref[...]

    def lower_bound(target):
        def step(_, lohi):
            lo, hi = lohi
            mid = (lo + hi) // 2
            below = idx_ref[base + jnp.minimum(mid, cap - 1)] < target
            active = lo < hi
            return (jnp.where(active & below, mid + 1, lo),
                    jnp.where(active & jnp.logical_not(below), mid, hi))
        lo, _ = lax.fori_loop(0, cap.bit_length() + 1, step, (jnp.int32(0), jnp.int32(cap)))
        return lo

    c0 = lower_bound(t0)
    c1 = lower_bound(t0 + COMBINE_ROWS)

    def add_rows(c, count):
        rows = [pl.ds(idx_ref[base + c + k] - t0, 1) for k in range(count)]
        sums = [o_ref[rows[k], :] + ye_ref[pl.ds(c + k, 1), :] for k in range(count)]
        for k in range(count):
            o_ref[rows[k], :] = sums[k]

    n_groups = (c1 - c0) // COMBINE_UNROLL

    def group(i, _):
        add_rows(c0 + i * COMBINE_UNROLL, COMBINE_UNROLL)
        return 0

    def single(c, _):
        add_rows(c, 1)
        return 0

    lax.fori_loop(0, n_groups, group, 0)
    lax.fori_loop(c0 + n_groups * COMBINE_UNROLL, c1, single, 0)

    @pl.when(e == n_exp - 1)
    def _():
        x = o_ref[...]
        ms = jnp.mean(x * x, axis=-1, keepdims=True)
        o_ref[...] = x * lax.rsqrt(ms + NORM_EPS) * w_ref[...]


def _combine(idx_flat, x1, ye, w, batch, seq, cap):
    n, d = x1.shape
    nq = seq // COMBINE_ROWS
    return pl.pallas_call(
        functools.partial(_combine_kernel, cap=cap, n_exp=N_EXPERTS),
        grid_spec=pltpu.PrefetchScalarGridSpec(
            num_scalar_prefetch=1,
            grid=(batch, nq, N_EXPERTS),
            in_specs=[pl.BlockSpec((COMBINE_ROWS, d), lambda b, q, e, idx: (b * nq + q, 0)),
                      pl.BlockSpec((None, cap, d), lambda b, q, e, idx: (e, b, 0)),
                      pl.BlockSpec((1, d), lambda b, q, e, idx: (0, 0))],
            out_specs=pl.BlockSpec((COMBINE_ROWS, d), lambda b, q, e, idx: (b * nq + q, 0))),
        out_shape=jax.ShapeDtypeStruct((n, d), F32),
        compiler_params=_params(("arbitrary", "arbitrary", "arbitrary"), 56),
        name="combine_norm",
    )(idx_flat, x1, ye, w.reshape(1, d))


def _rope_tables(seq):
    pos = jnp.arange(seq, dtype=F32)
    inv_freq = ROPE_THETA ** (-jnp.arange(0, DIFF_HD, 2, dtype=F32) / DIFF_HD)
    ang = pos[:, None] * inv_freq[None, :]
    cos, sin = jnp.cos(ang), jnp.sin(ang)
    reps = LANES // (DIFF_HD // 2)
    sign = jnp.tile(jnp.concatenate([-jnp.ones((DIFF_HD // 2,), F32), jnp.ones((DIFF_HD // 2,), F32)]),
                    LANES // DIFF_HD)
    return jnp.tile(cos, (1, reps)), jnp.tile(sin, (1, reps)) * sign[None, :]


def kernel(x, norm_mix_w, w_in, gla_gate_up_fwd, gla_gate_bias_fwd, gla_gate_up_bwd, gla_gate_bias_bwd, gla_norm_w, diff_lambda_q1, diff_lambda_k1, diff_lambda_q2, diff_lambda_k2, diff_subln_w, w_out, norm_ffn_w, w_router, w_gate_e, w_up_e, w_down_e, norm_final_w):
    batch, seq, d = x.shape
    depth = w_in.shape[0]
    assert depth == 1, "the combine stage applies the final norm, so it must follow the only layer"
    n = batch * seq
    kw = gla_gate_up_fwd.shape[2]
    dk = kw // GLA_HEADS
    dv = d // GLA_HEADS
    rank = gla_gate_up_fwd.shape[1]
    diff_heads = d // (2 * DIFF_HD)
    cap = CAPACITY_FACTOR * seq // N_EXPERTS
    cos_t, sin_t = _rope_tables(seq)

    o_z = 2 * kw + 2 * d
    o_dq = o_z + 2 * rank
    o_dv = o_dq + 2 * d
    o_gate = o_dv + d

    xf = x.reshape(n, d)
    for l in range(depth):
        w = w_in[l]
        h = _rmsnorm(xf, norm_mix_w[l], NORM_EPS, BF16)
        pm = _proj(h, w, 0, o_z, F32)
        z = _proj(h, w, o_z, LANES, F32)
        qk = _proj(h, w, o_dq, 2 * d, BF16,
                   rope=(cos_t, sin_t, d, DIFF_HD ** -0.5 * math.log2(math.e), seq))
        vd = _proj(h, w, o_dv, d, BF16)
        gates = _proj(h, w, o_gate, 2 * d, F32)

        zero_rows = lambda a, before: jnp.pad(a, ((before, LANES - rank - before), (0, 0))).astype(BF16)
        o_f, o_b = _gla(pm, z, zero_rows(gla_gate_up_fwd[l], 0), gla_gate_bias_fwd[l].reshape(1, kw),
                        zero_rows(gla_gate_up_bwd[l], rank), gla_gate_bias_bwd[l].reshape(1, kw),
                        batch, seq, dk, dv)

        lam_init = 0.8 - 0.6 * math.exp(-0.3 * l)
        lam_vecs = jnp.pad(jnp.stack([diff_lambda_q1[l], diff_lambda_k1[l],
                                      diff_lambda_q2[l], diff_lambda_k2[l]]).astype(F32),
                           ((0, 4), (0, LANES - DIFF_HD)))
        vt = jnp.transpose(vd.reshape(batch, seq, diff_heads, 2 * DIFF_HD), (0, 2, 3, 1))
        ones_rows = jnp.zeros((batch, diff_heads, ATTN_VROWS - 2 * DIFF_HD, seq), BF16).at[:, :, 0].set(1.0)
        vt = jnp.concatenate([vt, ones_rows], axis=2).reshape(batch * diff_heads, ATTN_VROWS, seq)
        yb = _diff_attn(qk, vt, lam_vecs, diff_subln_w[l], batch, seq, diff_heads, lam_init)

        wr = jnp.pad(w_router[l], ((0, 0), (0, LANES - N_EXPERTS))).astype(BF16)
        x1, h2, aff = _merge(o_f, o_b, pm, yb, gates, xf, w_out[l].astype(BF16),
                             jnp.tile(gla_norm_w[l], GLA_HEADS).reshape(1, d),
                             norm_ffn_w[l].reshape(1, d), wr, dv)

        aff3 = aff.reshape(batch, seq, LANES)
        afft = jnp.transpose(aff3[:, :, :N_EXPERTS], (0, 2, 1))
        idx, g = _topk(afft, aff3, cap)
        idx_flat = idx.reshape(batch * N_EXPERTS * cap)
        g_e = jnp.transpose(g, (1, 0, 2, 3)).reshape(N_EXPERTS, batch * cap, 1)

        xe = _dispatch(idx_flat, h2, batch, seq, cap)
        ye = _ffn(xe, w_gate_e[l], w_up_e[l], w_down_e[l], g_e)

        xf = _combine(idx_flat, x1, ye, norm_final_w, batch, seq, cap)
    return xf.reshape(batch, seq, d)
```

```python
import functools
import math

import jax
import jax.numpy as jnp
from jax import lax
from jax.experimental import pallas as pl
from jax.experimental.pallas import tpu as pltpu

F32 = jnp.float32
BF16 = jnp.bfloat16
I32 = jnp.int32
U32 = jnp.uint32

GLA_HEADS = 4
GLA_GATE_RANK = 16
GLA_GATE_NORMALIZER = 16.0
GLA_CHUNK = 64
DIFF_HD = 64
ROPE_THETA = 10000.0
N_EXPERTS = 16
CAPACITY_FACTOR = 2
NORM_EPS = 1e-6
SUBLN_EPS = 1e-5

LANES = 128
VMEM_PHYSICAL = 64 * 1024 * 1024


def _params(semantics, vmem_mb):
    return pltpu.CompilerParams(dimension_semantics=semantics,
                                vmem_limit_bytes=vmem_mb * 1024 * 1024)


def _dot(a, b):
    return jnp.dot(a, b, preferred_element_type=F32)


def _dot_nt(a, b):
    return lax.dot_general(a, b, (((1,), (1,)), ((), ())), preferred_element_type=F32)


def _dot_tn(a, b):
    return lax.dot_general(a, b, (((0,), (0,)), ((), ())), preferred_element_type=F32)


def _pack_halves(x):
    n = x.shape[1] // 2
    lo = pltpu.bitcast(x[:, :n].astype(F32), U32)
    hi = pltpu.bitcast(x[:, n:].astype(F32), U32)
    return (lo >> 16) | (hi & jnp.uint32(0xFFFF0000))


def _unpack_halves(p):
    lo = pltpu.bitcast(p << 16, F32).astype(BF16)
    hi = pltpu.bitcast(p & jnp.uint32(0xFFFF0000), F32).astype(BF16)
    return lo, hi


def _split3(x):
    hi = x.astype(BF16)
    r1 = x - hi.astype(F32)
    mid = r1.astype(BF16)
    lo = (r1 - mid.astype(F32)).astype(BF16)
    return hi, mid, lo


def _rmsnorm_kernel(x_ref, w_ref, o_ref, *, eps):
    x = x_ref[...]
    ms = jnp.mean(x * x, axis=-1, keepdims=True)
    o_ref[...] = (x * lax.rsqrt(ms + eps) * w_ref[...]).astype(o_ref.dtype)


def _rmsnorm(x, w, eps, out_dtype, tm=512):
    n, d = x.shape
    return pl.pallas_call(
        functools.partial(_rmsnorm_kernel, eps=eps),
        grid=(n // tm,),
        in_specs=[pl.BlockSpec((tm, d), lambda i: (i, 0)),
                  pl.BlockSpec((1, d), lambda i: (0, 0))],
        out_specs=pl.BlockSpec((tm, d), lambda i: (i, 0)),
        out_shape=jax.ShapeDtypeStruct((n, d), out_dtype),
        compiler_params=_params(("parallel",), 32),
        name="rmsnorm",
    )(x, w.reshape(1, d))


PROJ_ROW_CHUNK = 256


def _proj_kernel(*refs, shift, rope, n_q_tiles, q_scale):
    h_ref, wa_ref = refs[0], refs[1]
    pos = 2
    wb_ref = None
    if shift:
        wb_ref = refs[pos]
        pos += 1
    if rope:
        cos_ref, sin_ref = refs[pos], refs[pos + 1]
        pos += 2
    o_ref, w_scr = refs[pos], refs[pos + 1]
    k, tn = wa_ref.shape

    @pl.when(pl.program_id(1) == 0)
    def _():
        for r in range(0, k, PROJ_ROW_CHUNK):
            rows = slice(r, r + PROJ_ROW_CHUNK)
            if shift:
                wide = jnp.concatenate([wa_ref[rows, :], wb_ref[rows, :]], axis=1)
                w_scr[rows, :] = pltpu.roll(wide, tn + LANES - shift, 1)[:, :tn].astype(BF16)
            else:
                w_scr[rows, :] = wa_ref[rows, :].astype(BF16)

    y = _dot(h_ref[...], w_scr[...])
    if not rope:
        o_ref[...] = y.astype(o_ref.dtype)
        return
    tm = y.shape[0]
    scale = jnp.where(pl.program_id(0) < n_q_tiles, q_scale, 1.0).astype(F32)
    cos = cos_ref[...]
    sin = sin_ref[...]
    lane = lax.broadcasted_iota(I32, (tm, LANES), 1)
    first_half = (lane & (DIFF_HD - 1)) < (DIFF_HD // 2)
    for c in range(tn // LANES):
        yc = y[:, c * LANES:(c + 1) * LANES]
        sw = jnp.where(first_half,
                       pltpu.roll(yc, LANES - DIFF_HD // 2, 1),
                       pltpu.roll(yc, DIFF_HD // 2, 1))
        o_ref[:, c * LANES:(c + 1) * LANES] = ((yc * cos + sw * sin) * scale).astype(o_ref.dtype)


def _proj(h, w, col0, width, out_dtype, tm=1024, tn=1024, rope=None):
    n, k = h.shape
    tn = min(tn, width)
    shift = col0 % LANES
    base = col0 - shift
    assert base % tn == 0 and width % tn == 0
    in_specs = [pl.BlockSpec((tm, k), lambda j, i: (i, 0)),
                pl.BlockSpec((k, tn), lambda j, i: (0, base // tn + j))]
    args = [h, w]
    if shift:
        in_specs.append(pl.BlockSpec((k, LANES), lambda j, i: (0, (base + (j + 1) * tn) // LANES)))
        args.append(w)
    n_q_tiles, q_scale = 0, 1.0
    if rope is not None:
        cos, sin, n_q_cols, q_scale, seq = rope
        n_q_tiles = n_q_cols // tn
        blocks_per_seq = seq // tm
        in_specs += [pl.BlockSpec((tm, LANES), lambda j, i: (i % blocks_per_seq, 0)),
                     pl.BlockSpec((tm, LANES), lambda j, i: (i % blocks_per_seq, 0))]
        args += [cos, sin]
    return pl.pallas_call(
        functools.partial(_proj_kernel, shift=shift, rope=rope is not None,
                          n_q_tiles=n_q_tiles, q_scale=q_scale),
        grid=(width // tn, n // tm),
        in_specs=in_specs,
        out_specs=pl.BlockSpec((tm, tn), lambda j, i: (i, j)),
        out_shape=jax.ShapeDtypeStruct((n, width), out_dtype),
        scratch_shapes=[pltpu.VMEM((k, tn), BF16)],
        compiler_params=_params(("parallel", "arbitrary"), 52),
        name="proj_rope" if rope is not None else "proj",
    )(*args)


GLA_BLOCK = 256
GLA_HEADS_PER_STEP = 4


def _log_sigmoid(x):
    return jnp.minimum(x, 0.0) - jnp.log1p(jnp.exp(-jnp.abs(x)))


def _gla_block(q, k, v, g, st_ref, forward):
    L = GLA_CHUNK
    n = GLA_BLOCK
    n_chunks = n // L
    row = lax.broadcasted_iota(I32, (n, n), 0)
    col = lax.broadcasted_iota(I32, (n, n), 1)
    rowc = row >> (L.bit_length() - 1)
    colc = col >> (L.bit_length() - 1)
    same = rowc == colc
    if forward:
        tri = jnp.where(same & (col <= row), 1.0, 0.0).astype(BF16)
        diag_mask = same & (col <= row)
        order = list(range(n_chunks))
        mid, last = L // 2 - 1, L - 1
    else:
        tri = jnp.where(same & (col >= row), 1.0, 0.0).astype(BF16)
        diag_mask = same & (col > row)
        order = list(range(n_chunks - 1, -1, -1))
        mid, last = L // 2, 0

    g_hi, g_mid, g_lo = _split3(g)
    b = _dot(tri, g_hi) + _dot(tri, g_mid) + _dot(tri, g_lo)
    dk = b.shape[1]
    chunks = range(n_chunks)
    b_mid = [b[c * L + mid:c * L + mid + 1, :] for c in chunks]
    b_last = [b[c * L + last:c * L + last + 1, :] for c in chunks]
    scanned = {}
    run = jnp.zeros((1, dk), F32)
    for c in order:
        scanned[c] = run
        run = run + b_last[c]
    b_tot = run
    rows_of = lambda vals: jnp.concatenate([jnp.broadcast_to(x, (L, dk)) for x in vals], axis=0)

    mid_full = rows_of(b_mid)
    qe = q * jnp.exp2(b - mid_full)
    ki = k * jnp.exp2(mid_full - b)
    ke_loc = (ki * rows_of([jnp.exp2(b_last[c] - b_mid[c]) for c in chunks])).astype(BF16)
    chunk_rows = lambda a, c: a[c * L:(c + 1) * L, :]
    qx_parts, ke_parts = [], []
    for c_src in order[:-1]:
        ref = scanned[c_src] + b_last[c_src]
        later = [(c > c_src) if forward else (c < c_src) for c in chunks]
        qx_parts.append(jnp.concatenate(
            [(chunk_rows(qe, c) * jnp.exp2(b_mid[c] + scanned[c] - ref)).astype(BF16) if later[c]
             else jnp.zeros((L, dk), BF16) for c in chunks], axis=0))
        ke_parts.append(jnp.concatenate(
            [chunk_rows(ke_loc, c) if c == c_src else jnp.zeros((L, dk), BF16) for c in chunks], axis=0))
    cross = _dot_nt(jnp.concatenate(qx_parts, axis=1), jnp.concatenate(ke_parts, axis=1))
    p = jnp.where(diag_mask, _dot_nt(qe.astype(BF16), ki.astype(BF16)), cross)
    vb = v.astype(BF16)
    st = st_ref[...]
    qb = qe * rows_of([jnp.exp2(b_mid[c] + scanned[c]) for c in chunks])
    o = _dot(p.astype(BF16), vb) + _dot_nt(qb.astype(BF16), st.astype(BF16))
    ke = ki * rows_of([jnp.exp2(b_tot - scanned[c] - b_mid[c]) for c in chunks])
    st_ref[...] = st * jnp.exp2(b_tot) + _dot_tn(vb, ke.astype(BF16))
    return o


def _gla_kernel(qf_ref, kf_ref, vf_ref, zf_ref, qb_ref, kb_ref, vb_ref, zb_ref,
                upf_ref, biasf_ref, upb_ref, biasb_ref, of_ref, ob_ref, sf_ref, sb_ref,
                *, q_scale):
    @pl.when(pl.program_id(2) == 0)
    def _():
        sf_ref[...] = jnp.zeros_like(sf_ref)
        sb_ref[...] = jnp.zeros_like(sb_ref)

    def gate(z_ref, up_ref, bias_ref):
        z = z_ref[...]
        z = jnp.where(lax.broadcasted_iota(I32, z.shape, 1) < 2 * GLA_GATE_RANK, z, 0.0)
        pre = _dot(z.astype(BF16), up_ref[...]) + bias_ref[...]
        return _log_sigmoid(pre) * (math.log2(math.e) / GLA_GATE_NORMALIZER)

    g_f = gate(zf_ref, upf_ref, biasf_ref)
    g_b = gate(zb_ref, upb_ref, biasb_ref)

    dv, dk = sf_ref.shape[1:]
    for hh in range(GLA_HEADS_PER_STEP):
        ck = slice(hh * dk, (hh + 1) * dk)
        cv = slice(hh * dv, (hh + 1) * dv)
        of_ref[:, cv] = _gla_block(qf_ref[:, ck] * q_scale, kf_ref[:, ck], vf_ref[:, cv], g_f[:, ck],
                                   sf_ref.at[hh], True)
        ob_ref[:, cv] = _gla_block(qb_ref[:, ck] * q_scale, kb_ref[:, ck], vb_ref[:, cv], g_b[:, ck],
                                   sb_ref.at[hh], False)


def _gla(pm, z, upf, biasf, upb, biasb, batch, seq, dk, dv):
    n = pm.shape[0]
    nb = seq // GLA_BLOCK
    H = GLA_HEADS
    hs = GLA_HEADS_PER_STEP
    kw = H * dk
    fwd = lambda b, h, i: b * nb + i
    bwd = lambda b, h, i: b * nb + nb - 1 - i
    k_col = kw // (hs * dk)
    v_col = 2 * kw // (hs * dv)

    def specs(rowf):
        return [pl.BlockSpec((GLA_BLOCK, hs * dk), lambda b, h, i: (rowf(b, h, i), h)),
                pl.BlockSpec((GLA_BLOCK, hs * dk), lambda b, h, i: (rowf(b, h, i), k_col + h)),
                pl.BlockSpec((GLA_BLOCK, hs * dv), lambda b, h, i: (rowf(b, h, i), v_col + h)),
                pl.BlockSpec((GLA_BLOCK, LANES), lambda b, h, i: (rowf(b, h, i), 0))]

    w_specs = [pl.BlockSpec((LANES, hs * dk), lambda b, h, i: (0, h)),
               pl.BlockSpec((1, hs * dk), lambda b, h, i: (0, h))]
    out_shape = jax.ShapeDtypeStruct((n, H * dv), F32)
    return pl.pallas_call(
        functools.partial(_gla_kernel, q_scale=dk ** -0.5),
        grid=(batch, H // hs, nb),
        in_specs=specs(fwd) + specs(bwd) + w_specs + w_specs,
        out_specs=[pl.BlockSpec((GLA_BLOCK, hs * dv), lambda b, h, i: (fwd(b, h, i), h)),
                   pl.BlockSpec((GLA_BLOCK, hs * dv), lambda b, h, i: (bwd(b, h, i), h))],
        out_shape=[out_shape, out_shape],
        scratch_shapes=[pltpu.VMEM((hs, dv, dk), F32), pltpu.VMEM((hs, dv, dk), F32)],
        compiler_params=_params(("parallel", "parallel", "arbitrary"), 32),
        name="gla",
    )(pm, pm, pm, z, pm, pm, pm, z, upf, biasf, upb, biasb)


ATTN_TQ = 256
ATTN_TK = 256
ATTN_SUB = 64
ATTN_VROWS = 2 * DIFF_HD + 16
ATTN_VT_CHUNK = 512


def _diff_attn_kernel(q_ref, k_ref, v_ref, lam_ref, w_ref, o_ref, s_ref, vt_ref, *, lam_init, eps):
    seq = k_ref.shape[0]
    for r in range(0, seq, ATTN_VT_CHUNK):
        vt_ref[:2 * DIFF_HD, r:r + ATTN_VT_CHUNK] = v_ref[r:r + ATTN_VT_CHUNK, :].astype(F32).T.astype(BF16)
    pad_row = lax.broadcasted_iota(I32, (ATTN_VROWS - 2 * DIFF_HD, seq), 0)
    vt_ref[2 * DIFF_HD:, :] = jnp.where(pad_row == 0, 1.0, 0.0).astype(BF16)

    lv = lam_ref[...]
    lam = (jnp.exp(jnp.sum(lv[0:1] * lv[1:2], axis=-1, keepdims=True))
           - jnp.exp(jnp.sum(lv[2:3] * lv[3:4], axis=-1, keepdims=True)) + lam_init)
    lane = lax.broadcasted_iota(I32, (ATTN_TQ, 2 * DIFF_HD), 1)
    w_col = w_ref[...]

    n_tiles = seq // ATTN_TQ
    n_chunks = seq // ATTN_TK
    groups = ATTN_TK // 8

    def scores(tile, c):
        q = q_ref[pl.ds(pl.multiple_of(tile * ATTN_TQ, ATTN_TQ), ATTN_TQ), :]
        keep = (lane < DIFF_HD) if c == 0 else (lane >= DIFF_HD)
        qc = jnp.where(keep, q, jnp.zeros_like(q))
        mrun = jnp.full((8, ATTN_TQ), -jnp.inf, F32)
        for j in range(n_chunks):
            st = _dot_nt(k_ref[j * ATTN_TK:(j + 1) * ATTN_TK, :], qc)
            s_ref[c, j * ATTN_TK:(j + 1) * ATTN_TK, :] = st
            mrun = jnp.maximum(mrun, jnp.max(st.reshape(groups, 8, ATTN_TQ), axis=0))
        return jnp.max(mrun, axis=0, keepdims=True)

    def weighted_values(c, m):
        acc = jnp.zeros((vt_ref.shape[0], ATTN_TQ), F32)
        for j in range(n_chunks):
            pieces = [jnp.exp2(s_ref[c, r:r + ATTN_SUB, :] - m).astype(BF16)
                      for r in range(j * ATTN_TK, (j + 1) * ATTN_TK, ATTN_SUB)]
            acc = acc + _dot(vt_ref[:, j * ATTN_TK:(j + 1) * ATTN_TK], jnp.concatenate(pieces, axis=0))
        return acc[:2 * DIFF_HD] / acc[2 * DIFF_HD:2 * DIFF_HD + 1]

    def q_tile(i, m0):
        o0 = weighted_values(0, m0)
        m1 = scores(i, 1)
        o1 = weighted_values(1, m1)
        m0_next = scores(jnp.minimum(i + 1, n_tiles - 1), 0)
        o = o0 - lam * o1
        ms = jnp.mean(o * o, axis=0, keepdims=True)
        y = (o * lax.rsqrt(ms + eps) * w_col) * (1.0 - lam_init)
        o_ref[pl.ds(pl.multiple_of(i * ATTN_TQ, ATTN_TQ), ATTN_TQ), :] = y.T
        return m0_next

    lax.fori_loop(0, n_tiles, q_tile, scores(0, 0))


def _diff_attn(qk, v, lam_vecs, subln_w, batch, seq, heads, lam_init):
    n = qk.shape[0]
    vd = 2 * DIFF_HD
    return pl.pallas_call(
        functools.partial(_diff_attn_kernel, lam_init=lam_init, eps=SUBLN_EPS),
        grid=(batch, heads),
        in_specs=[pl.BlockSpec((seq, vd), lambda b, h: (b, h)),
                  pl.BlockSpec((seq, vd), lambda b, h: (b, heads + h)),
                  pl.BlockSpec((seq, vd), lambda b, h: (b, h)),
                  pl.BlockSpec((8, LANES), lambda b, h: (0, 0)),
                  pl.BlockSpec((vd, 1), lambda b, h: (0, 0))],
        out_specs=pl.BlockSpec((seq, vd), lambda b, h: (b, h)),
        out_shape=jax.ShapeDtypeStruct((n, heads * vd), F32),
        scratch_shapes=[pltpu.VMEM((2, seq, ATTN_TQ), F32), pltpu.VMEM((ATTN_VROWS, seq), BF16)],
        compiler_params=_params(("parallel", "parallel"), 32),
        name="diff_attn",
    )(qk, qk, v, lam_vecs, subln_w.reshape(vd, 1))


def _merge_kernel(of_ref, ob_ref, r_ref, yb_ref, ga_ref, gb_ref, x_ref, wout_ref, gnw_ref,
                  fnw_ref, wr_ref, x1_ref, h2_ref, aff_ref, *, dv, n_experts):
    o = of_ref[...] + ob_ref[...]
    segs = []
    for hh in range(o.shape[1] // dv):
        seg = o[:, hh * dv:(hh + 1) * dv]
        ms = jnp.mean(seg * seg, axis=-1, keepdims=True)
        segs.append(seg * lax.rsqrt(ms + NORM_EPS))
    on = jnp.concatenate(segs, axis=1) * gnw_ref[...]
    r = r_ref[...]
    ya = on * (r * jax.nn.sigmoid(r))
    merged = jax.nn.sigmoid(ga_ref[...]) * ya + jax.nn.sigmoid(gb_ref[...]) * yb_ref[...]
    x1 = x_ref[...] + _dot(merged.astype(BF16), wout_ref[...])
    x1_ref[...] = x1
    ms = jnp.mean(x1 * x1, axis=-1, keepdims=True)
    h2 = (x1 * lax.rsqrt(ms + NORM_EPS) * fnw_ref[...]).astype(BF16)
    h2_ref[...] = _pack_halves(h2)
    logits = _dot(h2, wr_ref[...])
    lane = lax.broadcasted_iota(I32, logits.shape, 1)
    logits = jnp.where(lane < n_experts, logits, -jnp.inf)
    e = jnp.exp(logits - jnp.max(logits, axis=-1, keepdims=True))
    aff_ref[...] = e / jnp.sum(e, axis=-1, keepdims=True)


def _merge(o_f, o_b, pm, yb, gates, x, wout, gnw, fnw, wr, dv, tm=128):
    n, d = x.shape
    r_col = (pm.shape[1] - d) // d
    row = lambda i: (i, 0)
    const = lambda i: (0, 0)
    return pl.pallas_call(
        functools.partial(_merge_kernel, dv=dv, n_experts=N_EXPERTS),
        grid=(n // tm,),
        in_specs=[pl.BlockSpec((tm, d), row), pl.BlockSpec((tm, d), row),
                  pl.BlockSpec((tm, d), lambda i: (i, r_col)),
                  pl.BlockSpec((tm, d), row),
                  pl.BlockSpec((tm, d), lambda i: (i, 0)), pl.BlockSpec((tm, d), lambda i: (i, 1)),
                  pl.BlockSpec((tm, d), row),
                  pl.BlockSpec((d, d), const), pl.BlockSpec((1, d), const),
                  pl.BlockSpec((1, d), const), pl.BlockSpec((d, LANES), const)],
        out_specs=[pl.BlockSpec((tm, d), row), pl.BlockSpec((tm, d // 2), row),
                   pl.BlockSpec((tm, LANES), row)],
        out_shape=[jax.ShapeDtypeStruct((n, d), F32), jax.ShapeDtypeStruct((n, d // 2), U32),
                   jax.ShapeDtypeStruct((n, LANES), F32)],
        compiler_params=_params(("parallel",), 56),
        name="merge_outproj",
    )(o_f, o_b, pm, yb, gates, gates, x, wout, gnw, fnw, wr)


TOPK_LANE_BLOCK = 256


def _exclusive_prefix_count(flags, strict_upper):
    e, t = flags.shape
    carry = jnp.zeros((e, 1), F32)
    parts = []
    for blk in range(t // TOPK_LANE_BLOCK):
        f = flags[:, blk * TOPK_LANE_BLOCK:(blk + 1) * TOPK_LANE_BLOCK]
        parts.append(_dot(f.astype(BF16), strict_upper) + carry)
        carry = carry + jnp.sum(f, axis=-1, keepdims=True)
    return jnp.concatenate(parts, axis=1)


def _topk_kernel(afft_ref, aff_ref, idx_ref, g_ref, posm_ref, *, cap):
    a = afft_ref[0]
    n_exp, t = a.shape
    bits = pltpu.bitcast(a, I32)

    def search(i, thr):
        cand = thr | jnp.left_shift(jnp.int32(1), 30 - i)
        cnt = jnp.sum(jnp.where(bits >= cand, 1.0, 0.0), axis=-1, keepdims=True)
        return jnp.where(cnt >= cap, cand, thr)

    thr = lax.fori_loop(0, 31, search, jnp.zeros((n_exp, 1), I32))
    gt = jnp.where(bits > thr, 1.0, 0.0)
    eq = jnp.where(bits == thr, 1.0, 0.0)
    need = cap - jnp.sum(gt, axis=-1, keepdims=True)

    r = lax.broadcasted_iota(I32, (TOPK_LANE_BLOCK, TOPK_LANE_BLOCK), 0)
    c = lax.broadcasted_iota(I32, (TOPK_LANE_BLOCK, TOPK_LANE_BLOCK), 1)
    strict_upper = jnp.where(r < c, 1.0, 0.0).astype(BF16)

    tie_rank = _exclusive_prefix_count(eq, strict_upper)
    sel = gt + eq * jnp.where(tie_rank < need, 1.0, 0.0)
    pos = _exclusive_prefix_count(sel, strict_upper)
    posm_ref[...] = jnp.where(sel > 0.5, pos, -1.0)

    av = aff_ref[0]
    hi, mid, lo = _split3(av)
    lane = lax.broadcasted_iota(I32, av.shape, 1)
    tok = lax.broadcasted_iota(I32, av.shape, 0)
    digits = jnp.where(lane == 0, (tok >> 6).astype(F32),
                       jnp.where(lane == 1, (tok & 63).astype(F32), 0.0))
    feat = (digits + pltpu.roll(hi.astype(F32), 2, 1) + pltpu.roll(mid.astype(F32), 2 + n_exp, 1)
            + pltpu.roll(lo.astype(F32), 2 + 2 * n_exp, 1)).astype(BF16)

    slot = lax.broadcasted_iota(I32, (cap, t), 0).astype(F32)
    out_lane = lax.broadcasted_iota(I32, (cap, LANES), 1)

    def per_expert(e, _):
        onehot = jnp.where(posm_ref[pl.ds(e, 1), :] == slot, 1.0, 0.0).astype(BF16)
        res = _dot(onehot, feat)
        idx_ref[0, e] = (res[:, 0:1] * 64.0 + res[:, 1:2]).astype(I32)
        mine = (out_lane == 2 + e) | (out_lane == 2 + n_exp + e) | (out_lane == 2 + 2 * n_exp + e)
        g_ref[0, e] = jnp.sum(jnp.where(mine, res, 0.0), axis=-1, keepdims=True)
        return 0

    lax.fori_loop(0, n_exp, per_expert, 0)


def _topk(afft, aff, cap):
    batch, n_exp, t = afft.shape
    return pl.pallas_call(
        functools.partial(_topk_kernel, cap=cap),
        grid=(batch,),
        in_specs=[pl.BlockSpec((1, n_exp, t), lambda b: (b, 0, 0)),
                  pl.BlockSpec((1, t, LANES), lambda b: (b, 0, 0))],
        out_specs=[pl.BlockSpec((1, n_exp, cap, 1), lambda b: (b, 0, 0, 0)),
                   pl.BlockSpec((1, n_exp, cap, 1), lambda b: (b, 0, 0, 0))],
        out_shape=[jax.ShapeDtypeStruct((batch, n_exp, cap, 1), I32),
                   jax.ShapeDtypeStruct((batch, n_exp, cap, 1), F32)],
        scratch_shapes=[pltpu.VMEM((n_exp, t), F32)],
        compiler_params=_params(("parallel",), 48),
        name="expert_topk",
    )(afft, aff)


def _dispatch_kernel(idx_ref, h_ref, o_ref, *, cap, n_exp):
    base = (pl.program_id(0) * n_exp + pl.program_id(1)) * cap

    def body(c, _):
        o_ref[pl.ds(c, 1), :] = h_ref[pl.ds(idx_ref[base + c], 1), :]
        return 0

    lax.fori_loop(0, cap, body, 0, unroll=8)


def _dispatch(idx_flat, hp, batch, seq, cap):
    width = hp.shape[1]
    return pl.pallas_call(
        functools.partial(_dispatch_kernel, cap=cap, n_exp=N_EXPERTS),
        grid_spec=pltpu.PrefetchScalarGridSpec(
            num_scalar_prefetch=1,
            grid=(batch, N_EXPERTS),
            in_specs=[pl.BlockSpec((seq, width), lambda b, e, idx: (b, 0))],
            out_specs=pl.BlockSpec((None, cap, width), lambda b, e, idx: (e, b, 0))),
        out_shape=jax.ShapeDtypeStruct((N_EXPERTS, batch * cap, width), hp.dtype),
        compiler_params=_params(("arbitrary", "arbitrary"), 48),
        name="dispatch",
    )(idx_flat, hp)


def _ffn_kernel(x_ref, wg_ref, wu_ref, wd_ref, g_ref, o_ref, xs_ref):
    f = pl.program_id(1)
    d = xs_ref.shape[1]

    @pl.when(f == 0)
    def _():
        lo, hi = _unpack_halves(x_ref[...])
        xs_ref[:, :d // 2] = lo
        xs_ref[:, d // 2:] = hi
        o_ref[...] = jnp.zeros_like(o_ref)

    x = xs_ref[...]
    a = _dot(x, wg_ref[...].astype(BF16))
    u = _dot(x, wu_ref[...].astype(BF16))
    hid = (a * jax.nn.sigmoid(a) * u).astype(BF16)
    o_ref[...] += _dot(hid, wd_ref[...].astype(BF16))

    @pl.when(f == pl.num_programs(1) - 1)
    def _():
        o_ref[...] = o_ref[...] * g_ref[...]


def _ffn(xe, wg, wu, wd, g, tf=256):
    n_exp, rows, half = xe.shape
    d = 2 * half
    dff = wg.shape[2]
    return pl.pallas_call(
        _ffn_kernel,
        grid=(n_exp, dff // tf),
        in_specs=[pl.BlockSpec((None, rows, half), lambda e, f: (e, 0, 0)),
                  pl.BlockSpec((None, d, tf), lambda e, f: (e, 0, f)),
                  pl.BlockSpec((None, d, tf), lambda e, f: (e, 0, f)),
                  pl.BlockSpec((None, tf, d), lambda e, f: (e, f, 0)),
                  pl.BlockSpec((None, rows, 1), lambda e, f: (e, 0, 0))],
        out_specs=pl.BlockSpec((None, rows, d), lambda e, f: (e, 0, 0)),
        out_shape=jax.ShapeDtypeStruct((n_exp, rows, d), F32),
        scratch_shapes=[pltpu.VMEM((rows, d), BF16)],
        compiler_params=_params(("parallel", "arbitrary"), 56),
        name="expert_ffn",
    )(xe, wg, wu, wd, g)


COMBINE_ROWS = 1024
COMBINE_UNROLL = 4


def _combine_kernel(idx_ref, x1_ref, ye_ref, w_ref, o_ref, *, cap, n_exp):
    b = pl.program_id(0)
    q = pl.program_id(1)
    e = pl.program_id(2)
    base = (b * n_exp + e) * cap
    t0 = q * COMBINE_ROWS

    @pl.when(e == 0)
    def _():
        o_ref[...] = x1_ref[...]

    def lower_bound(target):
        def step(_, lohi):
            lo, hi = lohi
            mid = (lo + hi) // 2
            below = idx_ref[base + jnp.minimum(mid, cap - 1)] < target
            active = lo < hi
            return (jnp.where(active & below, mid + 1, lo),
                    jnp.where(active & jnp.logical_not(below), mid, hi))
        lo, _ = lax.fori_loop(0, cap.bit_length() + 1, step, (jnp.int32(0), jnp.int32(cap)))
        return lo

    c0 = lower_bound(t0)
    c1 = lower_bound(t0 + COMBINE_ROWS)

    def add_rows(c, count):
        rows = [pl.ds(idx_ref[base + c + k] - t0, 1) for k in range(count)]
        sums = [o_ref[rows[k], :] + ye_ref[pl.ds(c + k, 1), :] for k in range(count)]
        for k in range(count):
            o_ref[rows[k], :] = sums[k]

    n_groups = (c1 - c0) // COMBINE_UNROLL

    def group(i, _):
        add_rows(c0 + i * COMBINE_UNROLL, COMBINE_UNROLL)
        return 0

    def single(c, _):
        add_rows(c, 1)
        return 0

    lax.fori_loop(0, n_groups, group, 0)
    lax.fori_loop(c0 + n_groups * COMBINE_UNROLL, c1, single, 0)

    @pl.when(e == n_exp - 1)
    def _():
        x = o_ref[...]
        ms = jnp.mean(x * x, axis=-1, keepdims=True)
        o_ref[...] = x * lax.rsqrt(ms + NORM_EPS) * w_ref[...]


def _combine(idx_flat, x1, ye, w, batch, seq, cap):
    n, d = x1.shape
    nq = seq // COMBINE_ROWS
    return pl.pallas_call(
        functools.partial(_combine_kernel, cap=cap, n_exp=N_EXPERTS),
        grid_spec=pltpu.PrefetchScalarGridSpec(
            num_scalar_prefetch=1,
            grid=(batch, nq, N_EXPERTS),
            in_specs=[pl.BlockSpec((COMBINE_ROWS, d), lambda b, q, e, idx: (b * nq + q, 0)),
                      pl.BlockSpec((None, cap, d), lambda b, q, e, idx: (e, b, 0)),
                      pl.BlockSpec((1, d), lambda b, q, e, idx: (0, 0))],
            out_specs=pl.BlockSpec((COMBINE_ROWS, d), lambda b, q, e, idx: (b * nq + q, 0))),
        out_shape=jax.ShapeDtypeStruct((n, d), F32),
        compiler_params=_params(("arbitrary", "arbitrary", "arbitrary"), 56),
        name="combine_norm",
    )(idx_flat, x1, ye, w.reshape(1, d))


def _rope_tables(seq):
    pos = jnp.arange(seq, dtype=F32)
    inv_freq = ROPE_THETA ** (-jnp.arange(0, DIFF_HD, 2, dtype=F32) / DIFF_HD)
    ang = pos[:, None] * inv_freq[None, :]
    cos, sin = jnp.cos(ang), jnp.sin(ang)
    reps = LANES // (DIFF_HD // 2)
    sign = jnp.tile(jnp.concatenate([-jnp.ones((DIFF_HD // 2,), F32), jnp.ones((DIFF_HD // 2,), F32)]),
                    LANES // DIFF_HD)
    return jnp.tile(cos, (1, reps)), jnp.tile(sin, (1, reps)) * sign[None, :]


def kernel(x, norm_mix_w, w_in, gla_gate_up_fwd, gla_gate_bias_fwd, gla_gate_up_bwd, gla_gate_bias_bwd, gla_norm_w, diff_lambda_q1, diff_lambda_k1, diff_lambda_q2, diff_lambda_k2, diff_subln_w, w_out, norm_ffn_w, w_router, w_gate_e, w_up_e, w_down_e, norm_final_w):
    batch, seq, d = x.shape
    depth = w_in.shape[0]
    assert depth == 1, "the combine stage applies the final norm, so it must follow the only layer"
    n = batch * seq
    kw = gla_gate_up_fwd.shape[2]
    dk = kw // GLA_HEADS
    dv = d // GLA_HEADS
    rank = gla_gate_up_fwd.shape[1]
    diff_heads = d // (2 * DIFF_HD)
    cap = CAPACITY_FACTOR * seq // N_EXPERTS
    cos_t, sin_t = _rope_tables(seq)

    o_z = 2 * kw + 2 * d
    o_dq = o_z + 2 * rank
    o_dv = o_dq + 2 * d
    o_gate = o_dv + d

    xf = x.reshape(n, d)
    for l in range(depth):
        w = w_in[l]
        h = _rmsnorm(xf, norm_mix_w[l], NORM_EPS, BF16)
        pm = _proj(h, w, 0, o_z, F32)
        z = _proj(h, w, o_z, LANES, F32)
        qk = _proj(h, w, o_dq, 2 * d, BF16,
                   rope=(cos_t, sin_t, d, DIFF_HD ** -0.5 * math.log2(math.e), seq))
        vd = _proj(h, w, o_dv, d, BF16)
        gates = _proj(h, w, o_gate, 2 * d, F32)

        zero_rows = lambda a, before: jnp.pad(a, ((before, LANES - rank - before), (0, 0))).astype(BF16)
        o_f, o_b = _gla(pm, z, zero_rows(gla_gate_up_fwd[l], 0), gla_gate_bias_fwd[l].reshape(1, kw),
                        zero_rows(gla_gate_up_bwd[l], rank), gla_gate_bias_bwd[l].reshape(1, kw),
                        batch, seq, dk, dv)

        lam_init = 0.8 - 0.6 * math.exp(-0.3 * l)
        lam_vecs = jnp.pad(jnp.stack([diff_lambda_q1[l], diff_lambda_k1[l],
                                      diff_lambda_q2[l], diff_lambda_k2[l]]).astype(F32),
                           ((0, 4), (0, LANES - DIFF_HD)))
        yb = _diff_attn(qk, vd, lam_vecs, diff_subln_w[l], batch, seq, diff_heads, lam_init)

        wr = jnp.pad(w_router[l], ((0, 0), (0, LANES - N_EXPERTS))).astype(BF16)
        x1, h2, aff = _merge(o_f, o_b, pm, yb, gates, xf, w_out[l].astype(BF16),
                             jnp.tile(gla_norm_w[l], GLA_HEADS).reshape(1, d),
                             norm_ffn_w[l].reshape(1, d), wr, dv)

        aff3 = aff.reshape(batch, seq, LANES)
        afft = jnp.transpose(aff3[:, :, :N_EXPERTS], (0, 2, 1))
        idx, g = _topk(afft, aff3, cap)
        idx_flat = idx.reshape(batch * N_EXPERTS * cap)
        g_e = jnp.transpose(g, (1, 0, 2, 3)).reshape(N_EXPERTS, batch * cap, 1)

        xe = _dispatch(idx_flat, h2, batch, seq, cap)
        ye = _ffn(xe, w_gate_e[l], w_up_e[l], w_down_e[l], g_e)

        xf = _combine(idx_flat, x1, ye, norm_final_w, batch, seq, cap)
    return xf.reshape(batch, seq, d)
```

```python
import functools
import math

import jax
import jax.numpy as jnp
from jax import lax
from jax.experimental import pallas as pl
from jax.experimental.pallas import tpu as pltpu

F32 = jnp.float32
BF16 = jnp.bfloat16
I32 = jnp.int32
U32 = jnp.uint32

GLA_HEADS = 4
GLA_GATE_RANK = 16
GLA_GATE_NORMALIZER = 16.0
GLA_CHUNK = 64
DIFF_HD = 64
ROPE_THETA = 10000.0
N_EXPERTS = 16
CAPACITY_FACTOR = 2
NORM_EPS = 1e-6
SUBLN_EPS = 1e-5

LANES = 128
VMEM_PHYSICAL = 64 * 1024 * 1024


def _params(semantics, vmem_mb):
    return pltpu.CompilerParams(dimension_semantics=semantics,
                                vmem_limit_bytes=vmem_mb * 1024 * 1024)


def _dot(a, b):
    return jnp.dot(a, b, preferred_element_type=F32)


def _dot_nt(a, b):
    return lax.dot_general(a, b, (((1,), (1,)), ((), ())), preferred_element_type=F32)


def _dot_tn(a, b):
    return lax.dot_general(a, b, (((0,), (0,)), ((), ())), preferred_element_type=F32)


def _pack_halves(x):
    n = x.shape[1] // 2
    lo = pltpu.bitcast(x[:, :n].astype(F32), U32)
    hi = pltpu.bitcast(x[:, n:].astype(F32), U32)
    return (lo >> 16) | (hi & jnp.uint32(0xFFFF0000))


def _unpack_halves(p):
    lo = pltpu.bitcast(p << 16, F32).astype(BF16)
    hi = pltpu.bitcast(p & jnp.uint32(0xFFFF0000), F32).astype(BF16)
    return lo, hi


def _split3(x):
    hi = x.astype(BF16)
    r1 = x - hi.astype(F32)
    mid = r1.astype(BF16)
    lo = (r1 - mid.astype(F32)).astype(BF16)
    return hi, mid, lo


def _rmsnorm_kernel(x_ref, w_ref, o_ref, *, eps):
    x = x_ref[...]
    ms = jnp.mean(x * x, axis=-1, keepdims=True)
    o_ref[...] = (x * lax.rsqrt(ms + eps) * w_ref[...]).astype(o_ref.dtype)


def _rmsnorm(x, w, eps, out_dtype, tm=512):
    n, d = x.shape
    return pl.pallas_call(
        functools.partial(_rmsnorm_kernel, eps=eps),
        grid=(n // tm,),
        in_specs=[pl.BlockSpec((tm, d), lambda i: (i, 0)),
                  pl.BlockSpec((1, d), lambda i: (0, 0))],
        out_specs=pl.BlockSpec((tm, d), lambda i: (i, 0)),
        out_shape=jax.ShapeDtypeStruct((n, d), out_dtype),
        compiler_params=_params(("parallel",), 32),
        name="rmsnorm",
    )(x, w.reshape(1, d))


def _proj_kernel(*refs, shift, rope, n_q_tiles, q_scale):
    h_ref, wa_ref = refs[0], refs[1]
    pos = 2
    wb_ref = None
    if shift:
        wb_ref = refs[pos]
        pos += 1
    if rope:
        cos_ref, sin_ref = refs[pos], refs[pos + 1]
        pos += 2
    o_ref, w_scr = refs[pos], refs[pos + 1]
    tn = wa_ref.shape[0]

    @pl.when(pl.program_id(1) == 0)
    def _():
        if shift:
            w_scr[:tn - shift, :] = wa_ref[shift:, :].astype(BF16)
            w_scr[tn - shift:, :] = wb_ref[...].astype(BF16)
        else:
            w_scr[...] = wa_ref[...].astype(BF16)

    y = _dot_nt(h_ref[...], w_scr[...])
    if not rope:
        o_ref[...] = y.astype(o_ref.dtype)
        return
    tm = y.shape[0]
    scale = jnp.where(pl.program_id(0) < n_q_tiles, q_scale, 1.0).astype(F32)
    cos = cos_ref[...]
    sin = sin_ref[...]
    lane = lax.broadcasted_iota(I32, (tm, LANES), 1)
    first_half = (lane & (DIFF_HD - 1)) < (DIFF_HD // 2)
    for c in range(tn // LANES):
        yc = y[:, c * LANES:(c + 1) * LANES]
        sw = jnp.where(first_half,
                       pltpu.roll(yc, LANES - DIFF_HD // 2, 1),
                       pltpu.roll(yc, DIFF_HD // 2, 1))
        o_ref[:, c * LANES:(c + 1) * LANES] = ((yc * cos + sw * sin) * scale).astype(o_ref.dtype)


BF16_SUBLANES = 16


def _proj(h, wt, col0, width, out_dtype, tm=1024, tn=1024, rope=None):
    n, k = h.shape
    tn = min(tn, width)
    shift = col0 % LANES
    base = col0 - shift
    assert base % tn == 0 and width % tn == 0 and shift % BF16_SUBLANES == 0
    in_specs = [pl.BlockSpec((tm, k), lambda j, i: (i, 0)),
                pl.BlockSpec((tn, k), lambda j, i: (base // tn + j, 0))]
    args = [h, wt]
    if shift:
        assert (base + tn) % shift == 0
        in_specs.append(pl.BlockSpec((shift, k), lambda j, i: ((base + (j + 1) * tn) // shift, 0)))
        args.append(wt)
    n_q_tiles, q_scale = 0, 1.0
    if rope is not None:
        cos, sin, n_q_cols, q_scale, seq = rope
        n_q_tiles = n_q_cols // tn
        blocks_per_seq = seq // tm
        in_specs += [pl.BlockSpec((tm, LANES), lambda j, i: (i % blocks_per_seq, 0)),
                     pl.BlockSpec((tm, LANES), lambda j, i: (i % blocks_per_seq, 0))]
        args += [cos, sin]
    return pl.pallas_call(
        functools.partial(_proj_kernel, shift=shift, rope=rope is not None,
                          n_q_tiles=n_q_tiles, q_scale=q_scale),
        grid=(width // tn, n // tm),
        in_specs=in_specs,
        out_specs=pl.BlockSpec((tm, tn), lambda j, i: (i, j)),
        out_shape=jax.ShapeDtypeStruct((n, width), out_dtype),
        scratch_shapes=[pltpu.VMEM((tn, k), BF16)],
        compiler_params=_params(("parallel", "arbitrary"), 52),
        name="proj_rope" if rope is not None else "proj",
    )(*args)


GLA_BLOCK = 256
GLA_HEADS_PER_STEP = 4


def _log_sigmoid(x):
    return jnp.minimum(x, 0.0) - jnp.log1p(jnp.exp(-jnp.abs(x)))


def _gla_block(q, k, v, g, st_ref, forward):
    L = GLA_CHUNK
    n = GLA_BLOCK
    n_chunks = n // L
    row = lax.broadcasted_iota(I32, (n, n), 0)
    col = lax.broadcasted_iota(I32, (n, n), 1)
    rowc = row >> (L.bit_length() - 1)
    colc = col >> (L.bit_length() - 1)
    same = rowc == colc
    if forward:
        tri = jnp.where(same & (col <= row), 1.0, 0.0).astype(BF16)
        diag_mask = same & (col <= row)
        order = list(range(n_chunks))
        mid, last = L // 2 - 1, L - 1
    else:
        tri = jnp.where(same & (col >= row), 1.0, 0.0).astype(BF16)
        diag_mask = same & (col > row)
        order = list(range(n_chunks - 1, -1, -1))
        mid, last = L // 2, 0

    g_hi, g_mid, g_lo = _split3(g)
    b = _dot(tri, g_hi) + _dot(tri, g_mid) + _dot(tri, g_lo)
    dk = b.shape[1]
    chunks = range(n_chunks)
    b_mid = [b[c * L + mid:c * L + mid + 1, :] for c in chunks]
    b_last = [b[c * L + last:c * L + last + 1, :] for c in chunks]
    scanned = {}
    run = jnp.zeros((1, dk), F32)
    for c in order:
        scanned[c] = run
        run = run + b_last[c]
    b_tot = run
    rows_of = lambda vals: jnp.concatenate([jnp.broadcast_to(x, (L, dk)) for x in vals], axis=0)

    mid_full = rows_of(b_mid)
    qe = q * jnp.exp2(b - mid_full)
    ki = k * jnp.exp2(mid_full - b)
    ke_loc = (ki * rows_of([jnp.exp2(b_last[c] - b_mid[c]) for c in chunks])).astype(BF16)
    chunk_rows = lambda a, c: a[c * L:(c + 1) * L, :]
    qx_parts, ke_parts = [], []
    for c_src in order[:-1]:
        ref = scanned[c_src] + b_last[c_src]
        later = [(c > c_src) if forward else (c < c_src) for c in chunks]
        qx_parts.append(jnp.concatenate(
            [(chunk_rows(qe, c) * jnp.exp2(b_mid[c] + scanned[c] - ref)).astype(BF16) if later[c]
             else jnp.zeros((L, dk), BF16) for c in chunks], axis=0))
        ke_parts.append(jnp.concatenate(
            [chunk_rows(ke_loc, c) if c == c_src else jnp.zeros((L, dk), BF16) for c in chunks], axis=0))
    cross = _dot_nt(jnp.concatenate(qx_parts, axis=1), jnp.concatenate(ke_parts, axis=1))
    p = jnp.where(diag_mask, _dot_nt(qe.astype(BF16), ki.astype(BF16)), cross)
    vb = v.astype(BF16)
    st = st_ref[...]
    qb = qe * rows_of([jnp.exp2(b_mid[c] + scanned[c]) for c in chunks])
    o = _dot(p.astype(BF16), vb) + _dot_nt(qb.astype(BF16), st.astype(BF16))
    ke = ki * rows_of([jnp.exp2(b_tot - scanned[c] - b_mid[c]) for c in chunks])
    st_ref[...] = st * jnp.exp2(b_tot) + _dot_tn(vb, ke.astype(BF16))
    return o


def _gla_kernel(qf_ref, kf_ref, vf_ref, zf_ref, qb_ref, kb_ref, vb_ref, zb_ref,
                upf_ref, biasf_ref, upb_ref, biasb_ref, of_ref, ob_ref, sf_ref, sb_ref,
                *, q_scale):
    @pl.when(pl.program_id(2) == 0)
    def _():
        sf_ref[...] = jnp.zeros_like(sf_ref)
        sb_ref[...] = jnp.zeros_like(sb_ref)

    def gate(z_ref, up_ref, bias_ref):
        z = z_ref[...]
        z = jnp.where(lax.broadcasted_iota(I32, z.shape, 1) < 2 * GLA_GATE_RANK, z, 0.0)
        pre = _dot(z.astype(BF16), up_ref[...]) + bias_ref[...]
        return _log_sigmoid(pre) * (math.log2(math.e) / GLA_GATE_NORMALIZER)

    g_f = gate(zf_ref, upf_ref, biasf_ref)
    g_b = gate(zb_ref, upb_ref, biasb_ref)

    dv, dk = sf_ref.shape[1:]
    for hh in range(GLA_HEADS_PER_STEP):
        ck = slice(hh * dk, (hh + 1) * dk)
        cv = slice(hh * dv, (hh + 1) * dv)
        of_ref[:, cv] = _gla_block(qf_ref[:, ck] * q_scale, kf_ref[:, ck], vf_ref[:, cv], g_f[:, ck],
                                   sf_ref.at[hh], True)
        ob_ref[:, cv] = _gla_block(qb_ref[:, ck] * q_scale, kb_ref[:, ck], vb_ref[:, cv], g_b[:, ck],
                                   sb_ref.at[hh], False)


def _gla(pm, z, upf, biasf, upb, biasb, batch, seq, dk, dv):
    n = pm.shape[0]
    nb = seq // GLA_BLOCK
    H = GLA_HEADS
    hs = GLA_HEADS_PER_STEP
    kw = H * dk
    fwd = lambda b, h, i: b * nb + i
    bwd = lambda b, h, i: b * nb + nb - 1 - i
    k_col = kw // (hs * dk)
    v_col = 2 * kw // (hs * dv)

    def specs(rowf):
        return [pl.BlockSpec((GLA_BLOCK, hs * dk), lambda b, h, i: (rowf(b, h, i), h)),
                pl.BlockSpec((GLA_BLOCK, hs * dk), lambda b, h, i: (rowf(b, h, i), k_col + h)),
                pl.BlockSpec((GLA_BLOCK, hs * dv), lambda b, h, i: (rowf(b, h, i), v_col + h)),
                pl.BlockSpec((GLA_BLOCK, LANES), lambda b, h, i: (rowf(b, h, i), 0))]

    w_specs = [pl.BlockSpec((LANES, hs * dk), lambda b, h, i: (0, h)),
               pl.BlockSpec((1, hs * dk), lambda b, h, i: (0, h))]
    out_shape = jax.ShapeDtypeStruct((n, H * dv), F32)
    return pl.pallas_call(
        functools.partial(_gla_kernel, q_scale=dk ** -0.5),
        grid=(batch, H // hs, nb),
        in_specs=specs(fwd) + specs(bwd) + w_specs + w_specs,
        out_specs=[pl.BlockSpec((GLA_BLOCK, hs * dv), lambda b, h, i: (fwd(b, h, i), h)),
                   pl.BlockSpec((GLA_BLOCK, hs * dv), lambda b, h, i: (bwd(b, h, i), h))],
        out_shape=[out_shape, out_shape],
        scratch_shapes=[pltpu.VMEM((hs, dv, dk), F32), pltpu.VMEM((hs, dv, dk), F32)],
        compiler_params=_params(("parallel", "parallel", "arbitrary"), 32),
        name="gla",
    )(pm, pm, pm, z, pm, pm, pm, z, upf, biasf, upb, biasb)


ATTN_TQ = 256
ATTN_TK = 256
ATTN_SUB = 64
ATTN_VROWS = 2 * DIFF_HD + 16
ATTN_VT_CHUNK = 512


def _diff_attn_kernel(q_ref, k_ref, v_ref, lam_ref, w_ref, o_ref, s_ref, vt_ref, *, lam_init, eps):
    seq = k_ref.shape[0]
    for r in range(0, seq, ATTN_VT_CHUNK):
        vt_ref[:2 * DIFF_HD, r:r + ATTN_VT_CHUNK] = v_ref[r:r + ATTN_VT_CHUNK, :].astype(F32).T.astype(BF16)
    pad_row = lax.broadcasted_iota(I32, (ATTN_VROWS - 2 * DIFF_HD, seq), 0)
    vt_ref[2 * DIFF_HD:, :] = jnp.where(pad_row == 0, 1.0, 0.0).astype(BF16)

    lv = lam_ref[...]
    lam = (jnp.exp(jnp.sum(lv[0:1] * lv[1:2], axis=-1, keepdims=True))
           - jnp.exp(jnp.sum(lv[2:3] * lv[3:4], axis=-1, keepdims=True)) + lam_init)
    lane = lax.broadcasted_iota(I32, (ATTN_TQ, 2 * DIFF_HD), 1)
    w_col = w_ref[...]

    n_tiles = seq // ATTN_TQ
    n_chunks = seq // ATTN_TK
    groups = ATTN_TK // 8

    def scores(tile, c):
        q = q_ref[pl.ds(pl.multiple_of(tile * ATTN_TQ, ATTN_TQ), ATTN_TQ), :]
        keep = (lane < DIFF_HD) if c == 0 else (lane >= DIFF_HD)
        qc = jnp.where(keep, q, jnp.zeros_like(q))
        mrun = jnp.full((8, ATTN_TQ), -jnp.inf, F32)
        for j in range(n_chunks):
            st = _dot_nt(k_ref[j * ATTN_TK:(j + 1) * ATTN_TK, :], qc)
            s_ref[c, j * ATTN_TK:(j + 1) * ATTN_TK, :] = st
            mrun = jnp.maximum(mrun, jnp.max(st.reshape(groups, 8, ATTN_TQ), axis=0))
        return jnp.max(mrun, axis=0, keepdims=True)

    def weighted_values(c, m):
        acc = jnp.zeros((vt_ref.shape[0], ATTN_TQ), F32)
        for j in range(n_chunks):
            pieces = [jnp.exp2(s_ref[c, r:r + ATTN_SUB, :] - m).astype(BF16)
                      for r in range(j * ATTN_TK, (j + 1) * ATTN_TK, ATTN_SUB)]
            acc = acc + _dot(vt_ref[:, j * ATTN_TK:(j + 1) * ATTN_TK], jnp.concatenate(pieces, axis=0))
        return acc[:2 * DIFF_HD] / acc[2 * DIFF_HD:2 * DIFF_HD + 1]

    def q_tile(i, m0):
        o0 = weighted_values(0, m0)
        m1 = scores(i, 1)
        o1 = weighted_values(1, m1)
        m0_next = scores(jnp.minimum(i + 1, n_tiles - 1), 0)
        o = o0 - lam * o1
        ms = jnp.mean(o * o, axis=0, keepdims=True)
        y = (o * lax.rsqrt(ms + eps) * w_col) * (1.0 - lam_init)
        o_ref[pl.ds(pl.multiple_of(i * ATTN_TQ, ATTN_TQ), ATTN_TQ), :] = y.T
        return m0_next

    lax.fori_loop(0, n_tiles, q_tile, scores(0, 0))


def _diff_attn(qk, v, lam_vecs, subln_w, batch, seq, heads, lam_init):
    n = qk.shape[0]
    vd = 2 * DIFF_HD
    return pl.pallas_call(
        functools.partial(_diff_attn_kernel, lam_init=lam_init, eps=SUBLN_EPS),
        grid=(batch, heads),
        in_specs=[pl.BlockSpec((seq, vd), lambda b, h: (b, h)),
                  pl.BlockSpec((seq, vd), lambda b, h: (b, heads + h)),
                  pl.BlockSpec((seq, vd), lambda b, h: (b, h)),
                  pl.BlockSpec((8, LANES), lambda b, h: (0, 0)),
                  pl.BlockSpec((vd, 1), lambda b, h: (0, 0))],
        out_specs=pl.BlockSpec((seq, vd), lambda b, h: (b, h)),
        out_shape=jax.ShapeDtypeStruct((n, heads * vd), F32),
        scratch_shapes=[pltpu.VMEM((2, seq, ATTN_TQ), F32), pltpu.VMEM((ATTN_VROWS, seq), BF16)],
        compiler_params=_params(("parallel", "parallel"), 32),
        name="diff_attn",
    )(qk, qk, v, lam_vecs, subln_w.reshape(vd, 1))


def _merge_kernel(of_ref, ob_ref, r_ref, yb_ref, ga_ref, gb_ref, x_ref, wout_ref, gnw_ref,
                  fnw_ref, wr_ref, x1_ref, h2_ref, aff_ref, *, dv, n_experts):
    o = of_ref[...] + ob_ref[...]
    segs = []
    for hh in range(o.shape[1] // dv):
        seg = o[:, hh * dv:(hh + 1) * dv]
        ms = jnp.mean(seg * seg, axis=-1, keepdims=True)
        segs.append(seg * lax.rsqrt(ms + NORM_EPS))
    on = jnp.concatenate(segs, axis=1) * gnw_ref[...]
    r = r_ref[...]
    ya = on * (r * jax.nn.sigmoid(r))
    merged = jax.nn.sigmoid(ga_ref[...]) * ya + jax.nn.sigmoid(gb_ref[...]) * yb_ref[...]
    x1 = x_ref[...] + _dot(merged.astype(BF16), wout_ref[...])
    x1_ref[...] = x1
    ms = jnp.mean(x1 * x1, axis=-1, keepdims=True)
    h2 = (x1 * lax.rsqrt(ms + NORM_EPS) * fnw_ref[...]).astype(BF16)
    h2_ref[...] = _pack_halves(h2)
    logits = _dot(h2, wr_ref[...])
    lane = lax.broadcasted_iota(I32, logits.shape, 1)
    logits = jnp.where(lane < n_experts, logits, -jnp.inf)
    e = jnp.exp(logits - jnp.max(logits, axis=-1, keepdims=True))
    aff_ref[...] = e / jnp.sum(e, axis=-1, keepdims=True)


def _merge(o_f, o_b, pm, yb, gates, x, wout, gnw, fnw, wr, dv, tm=128):
    n, d = x.shape
    r_col = (pm.shape[1] - d) // d
    row = lambda i: (i, 0)
    const = lambda i: (0, 0)
    return pl.pallas_call(
        functools.partial(_merge_kernel, dv=dv, n_experts=N_EXPERTS),
        grid=(n // tm,),
        in_specs=[pl.BlockSpec((tm, d), row), pl.BlockSpec((tm, d), row),
                  pl.BlockSpec((tm, d), lambda i: (i, r_col)),
                  pl.BlockSpec((tm, d), row),
                  pl.BlockSpec((tm, d), lambda i: (i, 0)), pl.BlockSpec((tm, d), lambda i: (i, 1)),
                  pl.BlockSpec((tm, d), row),
                  pl.BlockSpec((d, d), const), pl.BlockSpec((1, d), const),
                  pl.BlockSpec((1, d), const), pl.BlockSpec((d, LANES), const)],
        out_specs=[pl.BlockSpec((tm, d), row), pl.BlockSpec((tm, d // 2), row),
                   pl.BlockSpec((tm, LANES), row)],
        out_shape=[jax.ShapeDtypeStruct((n, d), F32), jax.ShapeDtypeStruct((n, d // 2), U32),
                   jax.ShapeDtypeStruct((n, LANES), F32)],
        compiler_params=_params(("parallel",), 56),
        name="merge_outproj",
    )(o_f, o_b, pm, yb, gates, gates, x, wout, gnw, fnw, wr)


TOPK_LANE_BLOCK = 256


def _exclusive_prefix_count(flags, strict_upper):
    e, t = flags.shape
    carry = jnp.zeros((e, 1), F32)
    parts = []
    for blk in range(t // TOPK_LANE_BLOCK):
        f = flags[:, blk * TOPK_LANE_BLOCK:(blk + 1) * TOPK_LANE_BLOCK]
        parts.append(_dot(f.astype(BF16), strict_upper) + carry)
        carry = carry + jnp.sum(f, axis=-1, keepdims=True)
    return jnp.concatenate(parts, axis=1)


def _topk_kernel(afft_ref, aff_ref, idx_ref, g_ref, posm_ref, *, cap):
    a = afft_ref[0]
    n_exp, t = a.shape
    bits = pltpu.bitcast(a, I32)

    def search(i, thr):
        cand = thr | jnp.left_shift(jnp.int32(1), 30 - i)
        cnt = jnp.sum(jnp.where(bits >= cand, 1.0, 0.0), axis=-1, keepdims=True)
        return jnp.where(cnt >= cap, cand, thr)

    thr = lax.fori_loop(0, 31, search, jnp.zeros((n_exp, 1), I32))
    gt = jnp.where(bits > thr, 1.0, 0.0)
    eq = jnp.where(bits == thr, 1.0, 0.0)
    need = cap - jnp.sum(gt, axis=-1, keepdims=True)

    r = lax.broadcasted_iota(I32, (TOPK_LANE_BLOCK, TOPK_LANE_BLOCK), 0)
    c = lax.broadcasted_iota(I32, (TOPK_LANE_BLOCK, TOPK_LANE_BLOCK), 1)
    strict_upper = jnp.where(r < c, 1.0, 0.0).astype(BF16)

    tie_rank = _exclusive_prefix_count(eq, strict_upper)
    sel = gt + eq * jnp.where(tie_rank < need, 1.0, 0.0)
    pos = _exclusive_prefix_count(sel, strict_upper)
    posm_ref[...] = jnp.where(sel > 0.5, pos, -1.0)

    av = aff_ref[0]
    hi, mid, lo = _split3(av)
    lane = lax.broadcasted_iota(I32, av.shape, 1)
    tok = lax.broadcasted_iota(I32, av.shape, 0)
    digits = jnp.where(lane == 0, (tok >> 6).astype(F32),
                       jnp.where(lane == 1, (tok & 63).astype(F32), 0.0))
    feat = (digits + pltpu.roll(hi.astype(F32), 2, 1) + pltpu.roll(mid.astype(F32), 2 + n_exp, 1)
            + pltpu.roll(lo.astype(F32), 2 + 2 * n_exp, 1)).astype(BF16)

    slot = lax.broadcasted_iota(I32, (cap, t), 0).astype(F32)
    out_lane = lax.broadcasted_iota(I32, (cap, LANES), 1)

    def per_expert(e, _):
        onehot = jnp.where(posm_ref[pl.ds(e, 1), :] == slot, 1.0, 0.0).astype(BF16)
        res = _dot(onehot, feat)
        idx_ref[0, e] = (res[:, 0:1] * 64.0 + res[:, 1:2]).astype(I32)
        mine = (out_lane == 2 + e) | (out_lane == 2 + n_exp + e) | (out_lane == 2 + 2 * n_exp + e)
        g_ref[0, e] = jnp.sum(jnp.where(mine, res, 0.0), axis=-1, keepdims=True)
        return 0

    lax.fori_loop(0, n_exp, per_expert, 0)


def _topk(afft, aff, cap):
    batch, n_exp, t = afft.shape
    return pl.pallas_call(
        functools.partial(_topk_kernel, cap=cap),
        grid=(batch,),
        in_specs=[pl.BlockSpec((1, n_exp, t), lambda b: (b, 0, 0)),
                  pl.BlockSpec((1, t, LANES), lambda b: (b, 0, 0))],
        out_specs=[pl.BlockSpec((1, n_exp, cap, 1), lambda b: (b, 0, 0, 0)),
                   pl.BlockSpec((1, n_exp, cap, 1), lambda b: (b, 0, 0, 0))],
        out_shape=[jax.ShapeDtypeStruct((batch, n_exp, cap, 1), I32),
                   jax.ShapeDtypeStruct((batch, n_exp, cap, 1), F32)],
        scratch_shapes=[pltpu.VMEM((n_exp, t), F32)],
        compiler_params=_params(("parallel",), 48),
        name="expert_topk",
    )(afft, aff)


def _dispatch_kernel(idx_ref, h_ref, o_ref, *, cap, n_exp):
    base = (pl.program_id(0) * n_exp + pl.program_id(1)) * cap

    def body(c, _):
        o_ref[pl.ds(c, 1), :] = h_ref[pl.ds(idx_ref[base + c], 1), :]
        return 0

    lax.fori_loop(0, cap, body, 0, unroll=8)


def _dispatch(idx_flat, hp, batch, seq, cap):
    width = hp.shape[1]
    return pl.pallas_call(
        functools.partial(_dispatch_kernel, cap=cap, n_exp=N_EXPERTS),
        grid_spec=pltpu.PrefetchScalarGridSpec(
            num_scalar_prefetch=1,
            grid=(batch, N_EXPERTS),
            in_specs=[pl.BlockSpec((seq, width), lambda b, e, idx: (b, 0))],
            out_specs=pl.BlockSpec((None, cap, width), lambda b, e, idx: (e, b, 0))),
        out_shape=jax.ShapeDtypeStruct((N_EXPERTS, batch * cap, width), hp.dtype),
        compiler_params=_params(("arbitrary", "arbitrary"), 48),
        name="dispatch",
    )(idx_flat, hp)


def _ffn_kernel(x_ref, wg_ref, wu_ref, wd_ref, g_ref, o_ref, xs_ref):
    f = pl.program_id(1)
    d = xs_ref.shape[1]

    @pl.when(f == 0)
    def _():
        lo, hi = _unpack_halves(x_ref[...])
        xs_ref[:, :d // 2] = lo
        xs_ref[:, d // 2:] = hi
        o_ref[...] = jnp.zeros_like(o_ref)

    x = xs_ref[...]
    a = _dot(x, wg_ref[...].astype(BF16))
    u = _dot(x, wu_ref[...].astype(BF16))
    hid = (a * jax.nn.sigmoid(a) * u).astype(BF16)
    o_ref[...] += _dot(hid, wd_ref[...].astype(BF16))

    @pl.when(f == pl.num_programs(1) - 1)
    def _():
        o_ref[...] = o_ref[...] * g_ref[...]


def _ffn(xe, wg, wu, wd, g, tf=512):
    n_exp, rows, half = xe.shape
    d = 2 * half
    dff = wg.shape[2]
    once = pl.Buffered(1)
    return pl.pallas_call(
        _ffn_kernel,
        grid=(n_exp, dff // tf),
        in_specs=[pl.BlockSpec((None, rows, half), lambda e, f: (e, 0, 0), pipeline_mode=once),
                  pl.BlockSpec((None, d, tf), lambda e, f: (e, 0, f)),
                  pl.BlockSpec((None, d, tf), lambda e, f: (e, 0, f)),
                  pl.BlockSpec((None, tf, d), lambda e, f: (e, f, 0)),
                  pl.BlockSpec((None, rows, 1), lambda e, f: (e, 0, 0), pipeline_mode=once)],
        out_specs=pl.BlockSpec((None, rows, d), lambda e, f: (e, 0, 0)),
        out_shape=jax.ShapeDtypeStruct((n_exp, rows, d), F32),
        scratch_shapes=[pltpu.VMEM((rows, d), BF16)],
        compiler_params=_params(("parallel", "arbitrary"), 60),
        name="expert_ffn",
    )(xe, wg, wu, wd, g)


COMBINE_UNROLL = 4


def _combine_kernel(idx_ref, x1_ref, ye_ref, w_ref, o_ref, acc_ref, *, cap, n_exp):
    b = pl.program_id(0)
    s = pl.program_id(1)
    chunk = o_ref.shape[0]

    @pl.when(s == 0)
    def _():
        acc_ref[...] = jnp.zeros_like(acc_ref)

    @pl.when(s < n_exp)
    def _():
        rows = pl.ds(pl.multiple_of(s * chunk, chunk), chunk)
        acc_ref[rows, :] = acc_ref[rows, :] + x1_ref[...]
        base = (b * n_exp + s) * cap

        def group(i, _):
            c = i * COMBINE_UNROLL
            dst = [pl.ds(idx_ref[base + c + k], 1) for k in range(COMBINE_UNROLL)]
            sums = [acc_ref[dst[k], :] + ye_ref[pl.ds(c + k, 1), :] for k in range(COMBINE_UNROLL)]
            for k in range(COMBINE_UNROLL):
                acc_ref[dst[k], :] = sums[k]
            return 0

        lax.fori_loop(0, cap // COMBINE_UNROLL, group, 0)

    @pl.when(s >= n_exp)
    def _():
        x = acc_ref[pl.ds(pl.multiple_of((s - n_exp) * chunk, chunk), chunk), :]
        ms = jnp.mean(x * x, axis=-1, keepdims=True)
        o_ref[...] = x * lax.rsqrt(ms + NORM_EPS) * w_ref[...]


def _combine(idx_flat, x1, ye, w, batch, seq, cap):
    n, d = x1.shape
    n_exp = N_EXPERTS
    chunk = seq // n_exp
    assert cap % COMBINE_UNROLL == 0
    return pl.pallas_call(
        functools.partial(_combine_kernel, cap=cap, n_exp=n_exp),
        grid_spec=pltpu.PrefetchScalarGridSpec(
            num_scalar_prefetch=1,
            grid=(batch, 2 * n_exp),
            in_specs=[pl.BlockSpec((chunk, d), lambda b, s, idx: (b * n_exp + jnp.minimum(s, n_exp - 1), 0)),
                      pl.BlockSpec((None, cap, d), lambda b, s, idx: (jnp.minimum(s, n_exp - 1), b, 0)),
                      pl.BlockSpec((1, d), lambda b, s, idx: (0, 0))],
            out_specs=pl.BlockSpec((chunk, d), lambda b, s, idx: (b * n_exp + jnp.maximum(s - n_exp, 0), 0)),
            scratch_shapes=[pltpu.VMEM((seq, d), F32)]),
        out_shape=jax.ShapeDtypeStruct((n, d), F32),
        compiler_params=_params(("arbitrary", "arbitrary"), 56),
        name="combine_norm",
    )(idx_flat, x1, ye, w.reshape(1, d))


def _rope_tables(seq):
    pos = jnp.arange(seq, dtype=F32)
    inv_freq = ROPE_THETA ** (-jnp.arange(0, DIFF_HD, 2, dtype=F32) / DIFF_HD)
    ang = pos[:, None] * inv_freq[None, :]
    cos, sin = jnp.cos(ang), jnp.sin(ang)
    reps = LANES // (DIFF_HD // 2)
    sign = jnp.tile(jnp.concatenate([-jnp.ones((DIFF_HD // 2,), F32), jnp.ones((DIFF_HD // 2,), F32)]),
                    LANES // DIFF_HD)
    return jnp.tile(cos, (1, reps)), jnp.tile(sin, (1, reps)) * sign[None, :]


def kernel(x, norm_mix_w, w_in, gla_gate_up_fwd, gla_gate_bias_fwd, gla_gate_up_bwd, gla_gate_bias_bwd, gla_norm_w, diff_lambda_q1, diff_lambda_k1, diff_lambda_q2, diff_lambda_k2, diff_subln_w, w_out, norm_ffn_w, w_router, w_gate_e, w_up_e, w_down_e, norm_final_w):
    batch, seq, d = x.shape
    depth = w_in.shape[0]
    assert depth == 1, "the combine stage applies the final norm, so it must follow the only layer"
    n = batch * seq
    kw = gla_gate_up_fwd.shape[2]
    dk = kw // GLA_HEADS
    dv = d // GLA_HEADS
    rank = gla_gate_up_fwd.shape[1]
    diff_heads = d // (2 * DIFF_HD)
    cap = CAPACITY_FACTOR * seq // N_EXPERTS
    cos_t, sin_t = _rope_tables(seq)

    o_z = 2 * kw + 2 * d
    o_dq = o_z + 2 * rank
    o_dv = o_dq + 2 * d
    o_gate = o_dv + d

    xf = x.reshape(n, d)
    for l in range(depth):
        w = jnp.transpose(w_in[l])
        h = _rmsnorm(xf, norm_mix_w[l], NORM_EPS, BF16)
        pm = _proj(h, w, 0, o_z, F32)
        z = _proj(h, w, o_z, LANES, F32)
        qk = _proj(h, w, o_dq, 2 * d, BF16,
                   rope=(cos_t, sin_t, d, DIFF_HD ** -0.5 * math.log2(math.e), seq))
        vd = _proj(h, w, o_dv, d, BF16)
        gates = _proj(h, w, o_gate, 2 * d, F32)

        zero_rows = lambda a, before: jnp.pad(a, ((before, LANES - rank - before), (0, 0))).astype(BF16)
        o_f, o_b = _gla(pm, z, zero_rows(gla_gate_up_fwd[l], 0), gla_gate_bias_fwd[l].reshape(1, kw),
                        zero_rows(gla_gate_up_bwd[l], rank), gla_gate_bias_bwd[l].reshape(1, kw),
                        batch, seq, dk, dv)

        lam_init = 0.8 - 0.6 * math.exp(-0.3 * l)
        lam_vecs = jnp.pad(jnp.stack([diff_lambda_q1[l], diff_lambda_k1[l],
                                      diff_lambda_q2[l], diff_lambda_k2[l]]).astype(F32),
                           ((0, 4), (0, LANES - DIFF_HD)))
        yb = _diff_attn(qk, vd, lam_vecs, diff_subln_w[l], batch, seq, diff_heads, lam_init)

        wr = jnp.pad(w_router[l], ((0, 0), (0, LANES - N_EXPERTS))).astype(BF16)
        x1, h2, aff = _merge(o_f, o_b, pm, yb, gates, xf, w_out[l].astype(BF16),
                             jnp.tile(gla_norm_w[l], GLA_HEADS).reshape(1, d),
                             norm_ffn_w[l].reshape(1, d), wr, dv)

        aff3 = aff.reshape(batch, seq, LANES)
        afft = jnp.transpose(aff3[:, :, :N_EXPERTS], (0, 2, 1))
        idx, g = _topk(afft, aff3, cap)
        idx_flat = idx.reshape(batch * N_EXPERTS * cap)
        g_e = jnp.transpose(g, (1, 0, 2, 3)).reshape(N_EXPERTS, batch * cap, 1)

        xe = _dispatch(idx_flat, h2, batch, seq, cap)
        ye = _ffn(xe, w_gate_e[l], w_up_e[l], w_down_e[l], g_e)

        xf = _combine(idx_flat, x1, ye, norm_final_w, batch, seq, cap)
    return xf.reshape(batch, seq, d)
```

```python
import functools
import math

import jax
import jax.numpy as jnp
from jax import lax
from jax.experimental import pallas as pl
from jax.experimental.pallas import tpu as pltpu

F32 = jnp.float32
BF16 = jnp.bfloat16
I32 = jnp.int32
U32 = jnp.uint32

GLA_HEADS = 4
GLA_GATE_RANK = 16
GLA_GATE_NORMALIZER = 16.0
GLA_CHUNK = 64
DIFF_HD = 64
ROPE_THETA = 10000.0
N_EXPERTS = 16
CAPACITY_FACTOR = 2
NORM_EPS = 1e-6
SUBLN_EPS = 1e-5

LANES = 128
VMEM_PHYSICAL = 64 * 1024 * 1024


def _params(semantics, vmem_mb):
    return pltpu.CompilerParams(dimension_semantics=semantics,
                                vmem_limit_bytes=vmem_mb * 1024 * 1024)


def _dot(a, b):
    return jnp.dot(a, b, preferred_element_type=F32)


def _dot_nt(a, b):
    return lax.dot_general(a, b, (((1,), (1,)), ((), ())), preferred_element_type=F32)


def _dot_tn(a, b):
    return lax.dot_general(a, b, (((0,), (0,)), ((), ())), preferred_element_type=F32)


def _pack_halves(x):
    n = x.shape[1] // 2
    lo = pltpu.bitcast(x[:, :n].astype(F32), U32)
    hi = pltpu.bitcast(x[:, n:].astype(F32), U32)
    return (lo >> 16) | (hi & jnp.uint32(0xFFFF0000))


def _unpack_halves(p):
    lo = pltpu.bitcast(p << 16, F32).astype(BF16)
    hi = pltpu.bitcast(p & jnp.uint32(0xFFFF0000), F32).astype(BF16)
    return lo, hi


def _split3(x):
    hi = x.astype(BF16)
    r1 = x - hi.astype(F32)
    mid = r1.astype(BF16)
    lo = (r1 - mid.astype(F32)).astype(BF16)
    return hi, mid, lo


def _rmsnorm_kernel(x_ref, w_ref, o_ref, *, eps):
    x = x_ref[...]
    ms = jnp.mean(x * x, axis=-1, keepdims=True)
    o_ref[...] = (x * lax.rsqrt(ms + eps) * w_ref[...]).astype(o_ref.dtype)


def _rmsnorm(x, w, eps, out_dtype, tm=512):
    n, d = x.shape
    return pl.pallas_call(
        functools.partial(_rmsnorm_kernel, eps=eps),
        grid=(n // tm,),
        in_specs=[pl.BlockSpec((tm, d), lambda i: (i, 0)),
                  pl.BlockSpec((1, d), lambda i: (0, 0))],
        out_specs=pl.BlockSpec((tm, d), lambda i: (i, 0)),
        out_shape=jax.ShapeDtypeStruct((n, d), out_dtype),
        compiler_params=_params(("parallel",), 32),
        name="rmsnorm",
    )(x, w.reshape(1, d))


def _proj_kernel(*refs, shift, rope, n_q_tiles, q_scale):
    h_ref, wa_ref = refs[0], refs[1]
    pos = 2
    wb_ref = None
    if shift:
        wb_ref = refs[pos]
        pos += 1
    if rope:
        cos_ref, sin_ref = refs[pos], refs[pos + 1]
        pos += 2
    o_ref, w_scr = refs[pos], refs[pos + 1]
    tn = wa_ref.shape[0]

    @pl.when(pl.program_id(1) == 0)
    def _():
        if shift:
            w_scr[:tn - shift, :] = wa_ref[shift:, :].astype(BF16)
            w_scr[tn - shift:, :] = wb_ref[...].astype(BF16)
        else:
            w_scr[...] = wa_ref[...].astype(BF16)

    y = _dot_nt(h_ref[...], w_scr[...])
    if not rope:
        o_ref[...] = y.astype(o_ref.dtype)
        return
    tm = y.shape[0]
    scale = jnp.where(pl.program_id(0) < n_q_tiles, q_scale, 1.0).astype(F32)
    cos = cos_ref[...]
    sin = sin_ref[...]
    lane = lax.broadcasted_iota(I32, (tm, LANES), 1)
    first_half = (lane & (DIFF_HD - 1)) < (DIFF_HD // 2)
    for c in range(tn // LANES):
        yc = y[:, c * LANES:(c + 1) * LANES]
        sw = jnp.where(first_half,
                       pltpu.roll(yc, LANES - DIFF_HD // 2, 1),
                       pltpu.roll(yc, DIFF_HD // 2, 1))
        o_ref[:, c * LANES:(c + 1) * LANES] = ((yc * cos + sw * sin) * scale).astype(o_ref.dtype)


BF16_SUBLANES = 16


def _proj(h, wt, col0, width, out_dtype, tm=1024, tn=1024, rope=None):
    n, k = h.shape
    tn = min(tn, width)
    shift = col0 % LANES
    base = col0 - shift
    assert base % tn == 0 and width % tn == 0 and shift % BF16_SUBLANES == 0
    in_specs = [pl.BlockSpec((tm, k), lambda j, i: (i, 0)),
                pl.BlockSpec((tn, k), lambda j, i: (base // tn + j, 0))]
    args = [h, wt]
    if shift:
        assert (base + tn) % shift == 0
        in_specs.append(pl.BlockSpec((shift, k), lambda j, i: ((base + (j + 1) * tn) // shift, 0)))
        args.append(wt)
    n_q_tiles, q_scale = 0, 1.0
    if rope is not None:
        cos, sin, n_q_cols, q_scale, seq = rope
        n_q_tiles = n_q_cols // tn
        blocks_per_seq = seq // tm
        in_specs += [pl.BlockSpec((tm, LANES), lambda j, i: (i % blocks_per_seq, 0)),
                     pl.BlockSpec((tm, LANES), lambda j, i: (i % blocks_per_seq, 0))]
        args += [cos, sin]
    return pl.pallas_call(
        functools.partial(_proj_kernel, shift=shift, rope=rope is not None,
                          n_q_tiles=n_q_tiles, q_scale=q_scale),
        grid=(width // tn, n // tm),
        in_specs=in_specs,
        out_specs=pl.BlockSpec((tm, tn), lambda j, i: (i, j)),
        out_shape=jax.ShapeDtypeStruct((n, width), out_dtype),
        scratch_shapes=[pltpu.VMEM((tn, k), BF16)],
        compiler_params=_params(("parallel", "arbitrary"), 52),
        name="proj_rope" if rope is not None else "proj",
    )(*args)


GLA_BLOCK = 256
GLA_HEADS_PER_STEP = 4


def _log_sigmoid(x):
    return jnp.minimum(x, 0.0) - jnp.log1p(jnp.exp(-jnp.abs(x)))


def _gla_block(q, k, v, g, st_ref, forward):
    L = GLA_CHUNK
    n = GLA_BLOCK
    n_chunks = n // L
    row = lax.broadcasted_iota(I32, (n, n), 0)
    col = lax.broadcasted_iota(I32, (n, n), 1)
    rowc = row >> (L.bit_length() - 1)
    colc = col >> (L.bit_length() - 1)
    same = rowc == colc
    if forward:
        tri = jnp.where(same & (col <= row), 1.0, 0.0).astype(BF16)
        diag_mask = same & (col <= row)
        order = list(range(n_chunks))
        mid, last = L // 2 - 1, L - 1
    else:
        tri = jnp.where(same & (col >= row), 1.0, 0.0).astype(BF16)
        diag_mask = same & (col > row)
        order = list(range(n_chunks - 1, -1, -1))
        mid, last = L // 2, 0

    g_hi, g_mid, g_lo = _split3(g)
    b = _dot(tri, g_hi) + _dot(tri, g_mid) + _dot(tri, g_lo)
    dk = b.shape[1]
    chunks = range(n_chunks)
    b_mid = [b[c * L + mid:c * L + mid + 1, :] for c in chunks]
    b_last = [b[c * L + last:c * L + last + 1, :] for c in chunks]
    scanned = {}
    run = jnp.zeros((1, dk), F32)
    for c in order:
        scanned[c] = run
        run = run + b_last[c]
    b_tot = run
    rows_of = lambda vals: jnp.concatenate([jnp.broadcast_to(x, (L, dk)) for x in vals], axis=0)

    mid_full = rows_of(b_mid)
    qe = q * jnp.exp2(b - mid_full)
    ki = k * jnp.exp2(mid_full - b)
    ke_loc = (ki * rows_of([jnp.exp2(b_last[c] - b_mid[c]) for c in chunks])).astype(BF16)
    chunk_rows = lambda a, c: a[c * L:(c + 1) * L, :]
    qx_parts, ke_parts = [], []
    for c_src in order[:-1]:
        ref = scanned[c_src] + b_last[c_src]
        later = [(c > c_src) if forward else (c < c_src) for c in chunks]
        qx_parts.append(jnp.concatenate(
            [(chunk_rows(qe, c) * jnp.exp2(b_mid[c] + scanned[c] - ref)).astype(BF16) if later[c]
             else jnp.zeros((L, dk), BF16) for c in chunks], axis=0))
        ke_parts.append(jnp.concatenate(
            [chunk_rows(ke_loc, c) if c == c_src else jnp.zeros((L, dk), BF16) for c in chunks], axis=0))
    cross = _dot_nt(jnp.concatenate(qx_parts, axis=1), jnp.concatenate(ke_parts, axis=1))
    p = jnp.where(diag_mask, _dot_nt(qe.astype(BF16), ki.astype(BF16)), cross)
    vb = v.astype(BF16)
    st = st_ref[...]
    qb = qe * rows_of([jnp.exp2(b_mid[c] + scanned[c]) for c in chunks])
    o = _dot(p.astype(BF16), vb) + _dot_nt(qb.astype(BF16), st.astype(BF16))
    ke = ki * rows_of([jnp.exp2(b_tot - scanned[c] - b_mid[c]) for c in chunks])
    st_ref[...] = st * jnp.exp2(b_tot) + _dot_tn(vb, ke.astype(BF16))
    return o


def _gla_kernel(qf_ref, kf_ref, vf_ref, zf_ref, qb_ref, kb_ref, vb_ref, zb_ref,
                upf_ref, biasf_ref, upb_ref, biasb_ref, of_ref, ob_ref, sf_ref, sb_ref,
                *, q_scale):
    @pl.when(pl.program_id(2) == 0)
    def _():
        sf_ref[...] = jnp.zeros_like(sf_ref)
        sb_ref[...] = jnp.zeros_like(sb_ref)

    def gate(z_ref, up_ref, bias_ref):
        z = z_ref[...]
        z = jnp.where(lax.broadcasted_iota(I32, z.shape, 1) < 2 * GLA_GATE_RANK, z, 0.0)
        pre = _dot(z.astype(BF16), up_ref[...]) + bias_ref[...]
        return _log_sigmoid(pre) * (math.log2(math.e) / GLA_GATE_NORMALIZER)

    g_f = gate(zf_ref, upf_ref, biasf_ref)
    g_b = gate(zb_ref, upb_ref, biasb_ref)

    dv, dk = sf_ref.shape[1:]
    for hh in range(GLA_HEADS_PER_STEP):
        ck = slice(hh * dk, (hh + 1) * dk)
        cv = slice(hh * dv, (hh + 1) * dv)
        of_ref[:, cv] = _gla_block(qf_ref[:, ck] * q_scale, kf_ref[:, ck], vf_ref[:, cv], g_f[:, ck],
                                   sf_ref.at[hh], True)
        ob_ref[:, cv] = _gla_block(qb_ref[:, ck] * q_scale, kb_ref[:, ck], vb_ref[:, cv], g_b[:, ck],
                                   sb_ref.at[hh], False)


def _gla(pm, z, upf, biasf, upb, biasb, batch, seq, dk, dv):
    n = pm.shape[0]
    nb = seq // GLA_BLOCK
    H = GLA_HEADS
    hs = GLA_HEADS_PER_STEP
    kw = H * dk
    fwd = lambda b, h, i: b * nb + i
    bwd = lambda b, h, i: b * nb + nb - 1 - i
    k_col = kw // (hs * dk)
    v_col = 2 * kw // (hs * dv)

    def specs(rowf):
        return [pl.BlockSpec((GLA_BLOCK, hs * dk), lambda b, h, i: (rowf(b, h, i), h)),
                pl.BlockSpec((GLA_BLOCK, hs * dk), lambda b, h, i: (rowf(b, h, i), k_col + h)),
                pl.BlockSpec((GLA_BLOCK, hs * dv), lambda b, h, i: (rowf(b, h, i), v_col + h)),
                pl.BlockSpec((GLA_BLOCK, LANES), lambda b, h, i: (rowf(b, h, i), 0))]

    w_specs = [pl.BlockSpec((LANES, hs * dk), lambda b, h, i: (0, h)),
               pl.BlockSpec((1, hs * dk), lambda b, h, i: (0, h))]
    out_shape = jax.ShapeDtypeStruct((n, H * dv), F32)
    return pl.pallas_call(
        functools.partial(_gla_kernel, q_scale=dk ** -0.5),
        grid=(batch, H // hs, nb),
        in_specs=specs(fwd) + specs(bwd) + w_specs + w_specs,
        out_specs=[pl.BlockSpec((GLA_BLOCK, hs * dv), lambda b, h, i: (fwd(b, h, i), h)),
                   pl.BlockSpec((GLA_BLOCK, hs * dv), lambda b, h, i: (bwd(b, h, i), h))],
        out_shape=[out_shape, out_shape],
        scratch_shapes=[pltpu.VMEM((hs, dv, dk), F32), pltpu.VMEM((hs, dv, dk), F32)],
        compiler_params=_params(("parallel", "parallel", "arbitrary"), 32),
        name="gla",
    )(pm, pm, pm, z, pm, pm, pm, z, upf, biasf, upb, biasb)


ATTN_TQ = 256
ATTN_TK = 256
ATTN_SUB = 64
ATTN_VROWS = 2 * DIFF_HD + 16
ATTN_VT_CHUNK = 512
ATTN_HEADS_PER_STEP = 2


def _diff_attn_kernel(q_ref, k_ref, v_ref, lam_ref, w_ref, o_ref, s_ref, vt_ref, *, lam_init, eps):
    seq = k_ref.shape[0]
    vd = 2 * DIFF_HD
    heads = range(ATTN_HEADS_PER_STEP)
    cols = [slice(h * vd, (h + 1) * vd) for h in heads]
    pad_row = lax.broadcasted_iota(I32, (ATTN_VROWS - vd, seq), 0)
    for h in heads:
        for r in range(0, seq, ATTN_VT_CHUNK):
            vt_ref[h, :vd, r:r + ATTN_VT_CHUNK] = v_ref[r:r + ATTN_VT_CHUNK, cols[h]].astype(F32).T.astype(BF16)
        vt_ref[h, vd:, :] = jnp.where(pad_row == 0, 1.0, 0.0).astype(BF16)

    lv = lam_ref[...]
    lam = (jnp.exp(jnp.sum(lv[0:1] * lv[1:2], axis=-1, keepdims=True))
           - jnp.exp(jnp.sum(lv[2:3] * lv[3:4], axis=-1, keepdims=True)) + lam_init)
    lane = lax.broadcasted_iota(I32, (ATTN_TQ, vd), 1)
    w_col = w_ref[...]

    n_tiles = seq // ATTN_TQ
    n_chunks = seq // ATTN_TK
    groups = ATTN_TK // 8

    def masked_q(h, tile, c):
        q = q_ref[pl.ds(pl.multiple_of(tile * ATTN_TQ, ATTN_TQ), ATTN_TQ), cols[h]]
        keep = (lane < DIFF_HD) if c == 0 else (lane >= DIFF_HD)
        return jnp.where(keep, q, jnp.zeros_like(q))

    def score_chunk(h, qc, c, j, mrun):
        st = _dot_nt(k_ref[j * ATTN_TK:(j + 1) * ATTN_TK, cols[h]], qc)
        s_ref[h, c, j * ATTN_TK:(j + 1) * ATTN_TK, :] = st
        return jnp.maximum(mrun, jnp.max(st.reshape(groups, 8, ATTN_TQ), axis=0))

    def value_chunk(h, c, j, m, acc):
        pieces = [jnp.exp2(s_ref[h, c, r:r + ATTN_SUB, :] - m).astype(BF16)
                  for r in range(j * ATTN_TK, (j + 1) * ATTN_TK, ATTN_SUB)]
        return acc + _dot(vt_ref[h, :, j * ATTN_TK:(j + 1) * ATTN_TK], jnp.concatenate(pieces, axis=0))

    def scores(h, tile, c):
        qc = masked_q(h, tile, c)
        mrun = jnp.full((8, ATTN_TQ), -jnp.inf, F32)
        for j in range(n_chunks):
            mrun = score_chunk(h, qc, c, j, mrun)
        return jnp.max(mrun, axis=0, keepdims=True)

    def overlapped(h, c_val, m, tile, c_score):
        qc = masked_q(h, tile, c_score)
        mrun = jnp.full((8, ATTN_TQ), -jnp.inf, F32)
        acc = jnp.zeros((ATTN_VROWS, ATTN_TQ), F32)
        for j in range(n_chunks):
            acc = value_chunk(h, c_val, j, m, acc)
        for j in range(n_chunks):
            mrun = score_chunk(h, qc, c_score, j, mrun)
        o = acc[:vd] / acc[vd:vd + 1]
        return o, jnp.max(mrun, axis=0, keepdims=True)

    def q_tile(i, m0):
        nxt = jnp.minimum(i + 1, n_tiles - 1)
        first = [overlapped(h, 0, m0[h], i, 1) for h in heads]
        second = [overlapped(h, 1, first[h][1], nxt, 0) for h in heads]
        for h in heads:
            o = first[h][0] - lam * second[h][0]
            ms = jnp.mean(o * o, axis=0, keepdims=True)
            y = (o * lax.rsqrt(ms + eps) * w_col) * (1.0 - lam_init)
            o_ref[pl.ds(pl.multiple_of(i * ATTN_TQ, ATTN_TQ), ATTN_TQ), cols[h]] = y.T
        return tuple(second[h][1] for h in heads)

    lax.fori_loop(0, n_tiles, q_tile, tuple(scores(h, 0, 0) for h in heads))


def _diff_attn(qk, v, lam_vecs, subln_w, batch, seq, heads, lam_init):
    n = qk.shape[0]
    vd = 2 * DIFF_HD
    hs = ATTN_HEADS_PER_STEP
    steps = heads // hs
    return pl.pallas_call(
        functools.partial(_diff_attn_kernel, lam_init=lam_init, eps=SUBLN_EPS),
        grid=(batch, steps),
        in_specs=[pl.BlockSpec((seq, hs * vd), lambda b, h: (b, h)),
                  pl.BlockSpec((seq, hs * vd), lambda b, h: (b, steps + h)),
                  pl.BlockSpec((seq, hs * vd), lambda b, h: (b, h)),
                  pl.BlockSpec((8, LANES), lambda b, h: (0, 0)),
                  pl.BlockSpec((vd, 1), lambda b, h: (0, 0))],
        out_specs=pl.BlockSpec((seq, hs * vd), lambda b, h: (b, h)),
        out_shape=jax.ShapeDtypeStruct((n, heads * vd), F32),
        scratch_shapes=[pltpu.VMEM((hs, 2, seq, ATTN_TQ), F32), pltpu.VMEM((hs, ATTN_VROWS, seq), BF16)],
        compiler_params=_params(("parallel", "parallel"), 48),
        name="diff_attn",
    )(qk, qk, v, lam_vecs, subln_w.reshape(vd, 1))


def _merge_kernel(of_ref, ob_ref, r_ref, yb_ref, ga_ref, gb_ref, x_ref, wout_ref, gnw_ref,
                  fnw_ref, wr_ref, x1_ref, h2_ref, aff_ref, *, dv, n_experts):
    o = of_ref[...] + ob_ref[...]
    segs = []
    for hh in range(o.shape[1] // dv):
        seg = o[:, hh * dv:(hh + 1) * dv]
        ms = jnp.mean(seg * seg, axis=-1, keepdims=True)
        segs.append(seg * lax.rsqrt(ms + NORM_EPS))
    on = jnp.concatenate(segs, axis=1) * gnw_ref[...]
    r = r_ref[...]
    ya = on * (r * jax.nn.sigmoid(r))
    merged = jax.nn.sigmoid(ga_ref[...]) * ya + jax.nn.sigmoid(gb_ref[...]) * yb_ref[...]
    x1 = x_ref[...] + _dot(merged.astype(BF16), wout_ref[...])
    x1_ref[...] = x1
    ms = jnp.mean(x1 * x1, axis=-1, keepdims=True)
    h2 = (x1 * lax.rsqrt(ms + NORM_EPS) * fnw_ref[...]).astype(BF16)
    h2_ref[...] = _pack_halves(h2)
    logits = _dot(h2, wr_ref[...])
    lane = lax.broadcasted_iota(I32, logits.shape, 1)
    logits = jnp.where(lane < n_experts, logits, -jnp.inf)
    e = jnp.exp(logits - jnp.max(logits, axis=-1, keepdims=True))
    aff_ref[...] = e / jnp.sum(e, axis=-1, keepdims=True)


def _merge(o_f, o_b, pm, yb, gates, x, wout, gnw, fnw, wr, dv, tm=128):
    n, d = x.shape
    r_col = (pm.shape[1] - d) // d
    row = lambda i: (i, 0)
    const = lambda i: (0, 0)
    return pl.pallas_call(
        functools.partial(_merge_kernel, dv=dv, n_experts=N_EXPERTS),
        grid=(n // tm,),
        in_specs=[pl.BlockSpec((tm, d), row), pl.BlockSpec((tm, d), row),
                  pl.BlockSpec((tm, d), lambda i: (i, r_col)),
                  pl.BlockSpec((tm, d), row),
                  pl.BlockSpec((tm, d), lambda i: (i, 0)), pl.BlockSpec((tm, d), lambda i: (i, 1)),
                  pl.BlockSpec((tm, d), row),
                  pl.BlockSpec((d, d), const), pl.BlockSpec((1, d), const),
                  pl.BlockSpec((1, d), const), pl.BlockSpec((d, LANES), const)],
        out_specs=[pl.BlockSpec((tm, d), row), pl.BlockSpec((tm, d // 2), row),
                   pl.BlockSpec((tm, LANES), row)],
        out_shape=[jax.ShapeDtypeStruct((n, d), F32), jax.ShapeDtypeStruct((n, d // 2), U32),
                   jax.ShapeDtypeStruct((n, LANES), F32)],
        compiler_params=_params(("parallel",), 56),
        name="merge_outproj",
    )(o_f, o_b, pm, yb, gates, gates, x, wout, gnw, fnw, wr)


TOPK_LANE_BLOCK = 256


def _exclusive_prefix_count(flags, strict_upper):
    e, t = flags.shape
    carry = jnp.zeros((e, 1), F32)
    parts = []
    for blk in range(t // TOPK_LANE_BLOCK):
        f = flags[:, blk * TOPK_LANE_BLOCK:(blk + 1) * TOPK_LANE_BLOCK]
        parts.append(_dot(f.astype(BF16), strict_upper) + carry)
        carry = carry + jnp.sum(f, axis=-1, keepdims=True)
    return jnp.concatenate(parts, axis=1)


def _topk_kernel(afft_ref, aff_ref, idx_ref, g_ref, posm_ref, *, cap):
    a = afft_ref[0]
    n_exp, t = a.shape
    bits = pltpu.bitcast(a, I32)

    def search(i, thr):
        cand = thr | jnp.left_shift(jnp.int32(1), 30 - i)
        cnt = jnp.sum(jnp.where(bits >= cand, 1.0, 0.0), axis=-1, keepdims=True)
        return jnp.where(cnt >= cap, cand, thr)

    thr = lax.fori_loop(0, 31, search, jnp.zeros((n_exp, 1), I32))
    gt = jnp.where(bits > thr, 1.0, 0.0)
    eq = jnp.where(bits == thr, 1.0, 0.0)
    need = cap - jnp.sum(gt, axis=-1, keepdims=True)

    r = lax.broadcasted_iota(I32, (TOPK_LANE_BLOCK, TOPK_LANE_BLOCK), 0)
    c = lax.broadcasted_iota(I32, (TOPK_LANE_BLOCK, TOPK_LANE_BLOCK), 1)
    strict_upper = jnp.where(r < c, 1.0, 0.0).astype(BF16)

    tie_rank = _exclusive_prefix_count(eq, strict_upper)
    sel = gt + eq * jnp.where(tie_rank < need, 1.0, 0.0)
    pos = _exclusive_prefix_count(sel, strict_upper)
    posm_ref[...] = jnp.where(sel > 0.5, pos, -1.0)

    av = aff_ref[0]
    hi, mid, lo = _split3(av)
    lane = lax.broadcasted_iota(I32, av.shape, 1)
    tok = lax.broadcasted_iota(I32, av.shape, 0)
    digits = jnp.where(lane == 0, (tok >> 6).astype(F32),
                       jnp.where(lane == 1, (tok & 63).astype(F32), 0.0))
    feat = (digits + pltpu.roll(hi.astype(F32), 2, 1) + pltpu.roll(mid.astype(F32), 2 + n_exp, 1)
            + pltpu.roll(lo.astype(F32), 2 + 2 * n_exp, 1)).astype(BF16)

    slot = lax.broadcasted_iota(I32, (cap, t), 0).astype(F32)
    out_lane = lax.broadcasted_iota(I32, (cap, LANES), 1)

    def per_expert(e, _):
        onehot = jnp.where(posm_ref[pl.ds(e, 1), :] == slot, 1.0, 0.0).astype(BF16)
        res = _dot(onehot, feat)
        idx_ref[0, e] = (res[:, 0:1] * 64.0 + res[:, 1:2]).astype(I32)
        mine = (out_lane == 2 + e) | (out_lane == 2 + n_exp + e) | (out_lane == 2 + 2 * n_exp + e)
        g_ref[0, e] = jnp.sum(jnp.where(mine, res, 0.0), axis=-1, keepdims=True)
        return 0

    lax.fori_loop(0, n_exp, per_expert, 0, unroll=2)


def _topk(afft, aff, cap):
    batch, n_exp, t = afft.shape
    return pl.pallas_call(
        functools.partial(_topk_kernel, cap=cap),
        grid=(batch,),
        in_specs=[pl.BlockSpec((1, n_exp, t), lambda b: (b, 0, 0)),
                  pl.BlockSpec((1, t, LANES), lambda b: (b, 0, 0))],
        out_specs=[pl.BlockSpec((1, n_exp, cap, 1), lambda b: (b, 0, 0, 0)),
                   pl.BlockSpec((1, n_exp, cap, 1), lambda b: (b, 0, 0, 0))],
        out_shape=[jax.ShapeDtypeStruct((batch, n_exp, cap, 1), I32),
                   jax.ShapeDtypeStruct((batch, n_exp, cap, 1), F32)],
        scratch_shapes=[pltpu.VMEM((n_exp, t), F32)],
        compiler_params=_params(("parallel",), 48),
        name="expert_topk",
    )(afft, aff)


def _dispatch_kernel(idx_ref, h_ref, o_ref, *, cap, n_exp):
    base = (pl.program_id(0) * n_exp + pl.program_id(1)) * cap

    def body(c, _):
        o_ref[pl.ds(c, 1), :] = h_ref[pl.ds(idx_ref[base + c], 1), :]
        return 0

    lax.fori_loop(0, cap, body, 0, unroll=8)


def _dispatch(idx_flat, hp, batch, seq, cap):
    width = hp.shape[1]
    return pl.pallas_call(
        functools.partial(_dispatch_kernel, cap=cap, n_exp=N_EXPERTS),
        grid_spec=pltpu.PrefetchScalarGridSpec(
            num_scalar_prefetch=1,
            grid=(batch, N_EXPERTS),
            in_specs=[pl.BlockSpec((seq, width), lambda b, e, idx: (b, 0))],
            out_specs=pl.BlockSpec((None, cap, width), lambda b, e, idx: (e, b, 0))),
        out_shape=jax.ShapeDtypeStruct((N_EXPERTS, batch * cap, width), hp.dtype),
        compiler_params=_params(("arbitrary", "arbitrary"), 48),
        name="dispatch",
    )(idx_flat, hp)


def _ffn_kernel(x_ref, wg_ref, wu_ref, wd_ref, g_ref, o_ref, xs_ref):
    f = pl.program_id(1)
    d = xs_ref.shape[1]

    @pl.when(f == 0)
    def _():
        lo, hi = _unpack_halves(x_ref[...])
        xs_ref[:, :d // 2] = lo
        xs_ref[:, d // 2:] = hi
        o_ref[...] = jnp.zeros_like(o_ref)

    x = xs_ref[...]
    a = _dot(x, wg_ref[...].astype(BF16))
    u = _dot(x, wu_ref[...].astype(BF16))
    hid = (a * jax.nn.sigmoid(a) * u).astype(BF16)
    o_ref[...] += _dot(hid, wd_ref[...].astype(BF16))

    @pl.when(f == pl.num_programs(1) - 1)
    def _():
        o_ref[...] = o_ref[...] * g_ref[...]


def _ffn(xe, wg, wu, wd, g, tf=256):
    n_exp, rows, half = xe.shape
    d = 2 * half
    dff = wg.shape[2]
    return pl.pallas_call(
        _ffn_kernel,
        grid=(n_exp, dff // tf),
        in_specs=[pl.BlockSpec((None, rows, half), lambda e, f: (e, 0, 0)),
                  pl.BlockSpec((None, d, tf), lambda e, f: (e, 0, f)),
                  pl.BlockSpec((None, d, tf), lambda e, f: (e, 0, f)),
                  pl.BlockSpec((None, tf, d), lambda e, f: (e, f, 0)),
                  pl.BlockSpec((None, rows, 1), lambda e, f: (e, 0, 0))],
        out_specs=pl.BlockSpec((None, rows, d), lambda e, f: (e, 0, 0)),
        out_shape=jax.ShapeDtypeStruct((n_exp, rows, d), F32),
        scratch_shapes=[pltpu.VMEM((rows, d), BF16)],
        compiler_params=_params(("parallel", "arbitrary"), 56),
        name="expert_ffn",
    )(xe, wg, wu, wd, g)


COMBINE_UNROLL = 8


def _combine_kernel(idx_ref, x1_ref, ye_ref, w_ref, o_ref, acc_ref, *, cap, n_exp):
    b = pl.program_id(0)
    s = pl.program_id(1)
    chunk = o_ref.shape[0]

    @pl.when(s == 0)
    def _():
        acc_ref[...] = jnp.zeros_like(acc_ref)

    @pl.when(s < n_exp)
    def _():
        rows = pl.ds(pl.multiple_of(s * chunk, chunk), chunk)
        acc_ref[rows, :] = acc_ref[rows, :] + x1_ref[...]
        base = (b * n_exp + s) * cap

        def group(i, _):
            c = i * COMBINE_UNROLL
            dst = [pl.ds(idx_ref[base + c + k], 1) for k in range(COMBINE_UNROLL)]
            sums = [acc_ref[dst[k], :] + ye_ref[pl.ds(c + k, 1), :] for k in range(COMBINE_UNROLL)]
            for k in range(COMBINE_UNROLL):
                acc_ref[dst[k], :] = sums[k]
            return 0

        lax.fori_loop(0, cap // COMBINE_UNROLL, group, 0)

    @pl.when(s >= n_exp)
    def _():
        x = acc_ref[pl.ds(pl.multiple_of((s - n_exp) * chunk, chunk), chunk), :]
        ms = jnp.mean(x * x, axis=-1, keepdims=True)
        o_ref[...] = x * lax.rsqrt(ms + NORM_EPS) * w_ref[...]


def _combine(idx_flat, x1, ye, w, batch, seq, cap):
    n, d = x1.shape
    n_exp = N_EXPERTS
    chunk = seq // n_exp
    assert cap % COMBINE_UNROLL == 0
    return pl.pallas_call(
        functools.partial(_combine_kernel, cap=cap, n_exp=n_exp),
        grid_spec=pltpu.PrefetchScalarGridSpec(
            num_scalar_prefetch=1,
            grid=(batch, 2 * n_exp),
            in_specs=[pl.BlockSpec((chunk, d), lambda b, s, idx: (b * n_exp + jnp.minimum(s, n_exp - 1), 0)),
                      pl.BlockSpec((None, cap, d), lambda b, s, idx: (jnp.minimum(s, n_exp - 1), b, 0)),
                      pl.BlockSpec((1, d), lambda b, s, idx: (0, 0))],
            out_specs=pl.BlockSpec((chunk, d), lambda b, s, idx: (b * n_exp + jnp.maximum(s - n_exp, 0), 0)),
            scratch_shapes=[pltpu.VMEM((seq, d), F32)]),
        out_shape=jax.ShapeDtypeStruct((n, d), F32),
        compiler_params=_params(("arbitrary", "arbitrary"), 56),
        name="combine_norm",
    )(idx_flat, x1, ye, w.reshape(1, d))


def _rope_tables(seq):
    pos = jnp.arange(seq, dtype=F32)
    inv_freq = ROPE_THETA ** (-jnp.arange(0, DIFF_HD, 2, dtype=F32) / DIFF_HD)
    ang = pos[:, None] * inv_freq[None, :]
    cos, sin = jnp.cos(ang), jnp.sin(ang)
    reps = LANES // (DIFF_HD // 2)
    sign = jnp.tile(jnp.concatenate([-jnp.ones((DIFF_HD // 2,), F32), jnp.ones((DIFF_HD // 2,), F32)]),
                    LANES // DIFF_HD)
    return jnp.tile(cos, (1, reps)), jnp.tile(sin, (1, reps)) * sign[None, :]


def kernel(x, norm_mix_w, w_in, gla_gate_up_fwd, gla_gate_bias_fwd, gla_gate_up_bwd, gla_gate_bias_bwd, gla_norm_w, diff_lambda_q1, diff_lambda_k1, diff_lambda_q2, diff_lambda_k2, diff_subln_w, w_out, norm_ffn_w, w_router, w_gate_e, w_up_e, w_down_e, norm_final_w):
    batch, seq, d = x.shape
    depth = w_in.shape[0]
    assert depth == 1, "the combine stage applies the final norm, so it must follow the only layer"
    n = batch * seq
    kw = gla_gate_up_fwd.shape[2]
    dk = kw // GLA_HEADS
    dv = d // GLA_HEADS
    rank = gla_gate_up_fwd.shape[1]
    diff_heads = d // (2 * DIFF_HD)
    cap = CAPACITY_FACTOR * seq // N_EXPERTS
    cos_t, sin_t = _rope_tables(seq)

    o_z = 2 * kw + 2 * d
    o_dq = o_z + 2 * rank
    o_dv = o_dq + 2 * d
    o_gate = o_dv + d

    xf = x.reshape(n, d)
    for l in range(depth):
        w = jnp.transpose(w_in[l])
        h = _rmsnorm(xf, norm_mix_w[l], NORM_EPS, BF16)
        pm = _proj(h, w, 0, o_z, F32)
        z = _proj(h, w, o_z, LANES, F32)
        qk = _proj(h, w, o_dq, 2 * d, BF16,
                   rope=(cos_t, sin_t, d, DIFF_HD ** -0.5 * math.log2(math.e), seq))
        vd = _proj(h, w, o_dv, d, BF16)
        gates = _proj(h, w, o_gate, 2 * d, F32)

        zero_rows = lambda a, before: jnp.pad(a, ((before, LANES - rank - before), (0, 0))).astype(BF16)
        o_f, o_b = _gla(pm, z, zero_rows(gla_gate_up_fwd[l], 0), gla_gate_bias_fwd[l].reshape(1, kw),
                        zero_rows(gla_gate_up_bwd[l], rank), gla_gate_bias_bwd[l].reshape(1, kw),
                        batch, seq, dk, dv)

        lam_init = 0.8 - 0.6 * math.exp(-0.3 * l)
        lam_vecs = jnp.pad(jnp.stack([diff_lambda_q1[l], diff_lambda_k1[l],
                                      diff_lambda_q2[l], diff_lambda_k2[l]]).astype(F32),
                           ((0, 4), (0, LANES - DIFF_HD)))
        yb = _diff_attn(qk, vd, lam_vecs, diff_subln_w[l], batch, seq, diff_heads, lam_init)

        wr = jnp.pad(w_router[l], ((0, 0), (0, LANES - N_EXPERTS))).astype(BF16)
        x1, h2, aff = _merge(o_f, o_b, pm, yb, gates, xf, w_out[l].astype(BF16),
                             jnp.tile(gla_norm_w[l], GLA_HEADS).reshape(1, d),
                             norm_ffn_w[l].reshape(1, d), wr, dv)

        aff3 = aff.reshape(batch, seq, LANES)
        afft = jnp.transpose(aff3[:, :, :N_EXPERTS], (0, 2, 1))
        idx, g = _topk(afft, aff3, cap)
        idx_flat = idx.reshape(batch * N_EXPERTS * cap)
        g_e = jnp.transpose(g, (1, 0, 2, 3)).reshape(N_EXPERTS, batch * cap, 1)

        xe = _dispatch(idx_flat, h2, batch, seq, cap)
        ye = _ffn(xe, w_gate_e[l], w_up_e[l], w_down_e[l], g_e)

        xf = _combine(idx_flat, x1, ye, norm_final_w, batch, seq, cap)
    return xf.reshape(batch, seq, d)
```

```python
import functools
import math

import jax
import jax.numpy as jnp
from jax import lax
from jax.experimental import pallas as pl
from jax.experimental.pallas import tpu as pltpu

F32 = jnp.float32
BF16 = jnp.bfloat16
I32 = jnp.int32
U32 = jnp.uint32

GLA_HEADS = 4
GLA_GATE_RANK = 16
GLA_GATE_NORMALIZER = 16.0
GLA_CHUNK = 64
DIFF_HD = 64
ROPE_THETA = 10000.0
N_EXPERTS = 16
CAPACITY_FACTOR = 2
NORM_EPS = 1e-6
SUBLN_EPS = 1e-5

LANES = 128
VMEM_PHYSICAL = 64 * 1024 * 1024


def _params(semantics, vmem_mb):
    return pltpu.CompilerParams(dimension_semantics=semantics,
                                vmem_limit_bytes=vmem_mb * 1024 * 1024)


def _dot(a, b):
    return jnp.dot(a, b, preferred_element_type=F32)


def _dot_nt(a, b):
    return lax.dot_general(a, b, (((1,), (1,)), ((), ())), preferred_element_type=F32)


def _dot_tn(a, b):
    return lax.dot_general(a, b, (((0,), (0,)), ((), ())), preferred_element_type=F32)


def _pack_halves(x):
    n = x.shape[1] // 2
    lo = pltpu.bitcast(x[:, :n].astype(F32), U32)
    hi = pltpu.bitcast(x[:, n:].astype(F32), U32)
    return (lo >> 16) | (hi & jnp.uint32(0xFFFF0000))


def _unpack_halves(p):
    lo = pltpu.bitcast(p << 16, F32).astype(BF16)
    hi = pltpu.bitcast(p & jnp.uint32(0xFFFF0000), F32).astype(BF16)
    return lo, hi


def _split3(x):
    hi = x.astype(BF16)
    r1 = x - hi.astype(F32)
    mid = r1.astype(BF16)
    lo = (r1 - mid.astype(F32)).astype(BF16)
    return hi, mid, lo


def _rmsnorm_kernel(x_ref, w_ref, o_ref, *, eps):
    x = x_ref[...]
    ms = jnp.mean(x * x, axis=-1, keepdims=True)
    o_ref[...] = (x * lax.rsqrt(ms + eps) * w_ref[...]).astype(o_ref.dtype)


def _rmsnorm(x, w, eps, out_dtype, tm=512):
    n, d = x.shape
    return pl.pallas_call(
        functools.partial(_rmsnorm_kernel, eps=eps),
        grid=(n // tm,),
        in_specs=[pl.BlockSpec((tm, d), lambda i: (i, 0)),
                  pl.BlockSpec((1, d), lambda i: (0, 0))],
        out_specs=pl.BlockSpec((tm, d), lambda i: (i, 0)),
        out_shape=jax.ShapeDtypeStruct((n, d), out_dtype),
        compiler_params=_params(("parallel",), 32),
        name="rmsnorm",
    )(x, w.reshape(1, d))


def _proj_kernel(*refs, shift, rope, n_q_tiles, q_scale):
    h_ref, wa_ref = refs[0], refs[1]
    pos = 2
    wb_ref = None
    if shift:
        wb_ref = refs[pos]
        pos += 1
    if rope:
        cos_ref, sin_ref = refs[pos], refs[pos + 1]
        pos += 2
    o_ref, w_scr = refs[pos], refs[pos + 1]
    tn = wa_ref.shape[0]

    @pl.when(pl.program_id(1) == 0)
    def _():
        if shift:
            w_scr[:tn - shift, :] = wa_ref[shift:, :].astype(BF16)
            w_scr[tn - shift:, :] = wb_ref[...].astype(BF16)
        else:
            w_scr[...] = wa_ref[...].astype(BF16)

    y = _dot_nt(h_ref[...], w_scr[...])
    if not rope:
        o_ref[...] = y.astype(o_ref.dtype)
        return
    tm = y.shape[0]
    scale = jnp.where(pl.program_id(0) < n_q_tiles, q_scale, 1.0).astype(F32)
    cos = cos_ref[...]
    sin = sin_ref[...]
    lane = lax.broadcasted_iota(I32, (tm, LANES), 1)
    first_half = (lane & (DIFF_HD - 1)) < (DIFF_HD // 2)
    for c in range(tn // LANES):
        yc = y[:, c * LANES:(c + 1) * LANES]
        sw = jnp.where(first_half,
                       pltpu.roll(yc, LANES - DIFF_HD // 2, 1),
                       pltpu.roll(yc, DIFF_HD // 2, 1))
        o_ref[:, c * LANES:(c + 1) * LANES] = ((yc * cos + sw * sin) * scale).astype(o_ref.dtype)


BF16_SUBLANES = 16


def _proj(h, wt, col0, width, out_dtype, tm=1024, tn=1024, rope=None):
    n, k = h.shape
    tn = min(tn, width)
    shift = col0 % LANES
    base = col0 - shift
    assert base % tn == 0 and width % tn == 0 and shift % BF16_SUBLANES == 0
    in_specs = [pl.BlockSpec((tm, k), lambda j, i: (i, 0)),
                pl.BlockSpec((tn, k), lambda j, i: (base // tn + j, 0))]
    args = [h, wt]
    if shift:
        assert (base + tn) % shift == 0
        in_specs.append(pl.BlockSpec((shift, k), lambda j, i: ((base + (j + 1) * tn) // shift, 0)))
        args.append(wt)
    n_q_tiles, q_scale = 0, 1.0
    if rope is not None:
        cos, sin, n_q_cols, q_scale, seq = rope
        n_q_tiles = n_q_cols // tn
        blocks_per_seq = seq // tm
        in_specs += [pl.BlockSpec((tm, LANES), lambda j, i: (i % blocks_per_seq, 0)),
                     pl.BlockSpec((tm, LANES), lambda j, i: (i % blocks_per_seq, 0))]
        args += [cos, sin]
    return pl.pallas_call(
        functools.partial(_proj_kernel, shift=shift, rope=rope is not None,
                          n_q_tiles=n_q_tiles, q_scale=q_scale),
        grid=(width // tn, n // tm),
        in_specs=in_specs,
        out_specs=pl.BlockSpec((tm, tn), lambda j, i: (i, j)),
        out_shape=jax.ShapeDtypeStruct((n, width), out_dtype),
        scratch_shapes=[pltpu.VMEM((tn, k), BF16)],
        compiler_params=_params(("parallel", "arbitrary"), 52),
        name="proj_rope" if rope is not None else "proj",
    )(*args)


GLA_BLOCK = 256
GLA_HEADS_PER_STEP = 4
GLA_STAGES = 3


def _log_sigmoid(x):
    return jnp.minimum(x, 0.0) - jnp.log1p(jnp.exp(-jnp.abs(x)))


def _gla_block(q, k, v, g, st_ref, forward):
    L = GLA_CHUNK
    n = GLA_BLOCK
    n_chunks = n // L
    row = lax.broadcasted_iota(I32, (n, n), 0)
    col = lax.broadcasted_iota(I32, (n, n), 1)
    rowc = row >> (L.bit_length() - 1)
    colc = col >> (L.bit_length() - 1)
    same = rowc == colc
    if forward:
        tri = jnp.where(same & (col <= row), 1.0, 0.0).astype(BF16)
        diag_mask = same & (col <= row)
        order = list(range(n_chunks))
        mid, last = L // 2 - 1, L - 1
    else:
        tri = jnp.where(same & (col >= row), 1.0, 0.0).astype(BF16)
        diag_mask = same & (col > row)
        order = list(range(n_chunks - 1, -1, -1))
        mid, last = L // 2, 0

    g_hi, g_mid, g_lo = _split3(g)
    b = _dot(tri, g_hi) + _dot(tri, g_mid) + _dot(tri, g_lo)
    yield None
    dk = b.shape[1]
    chunks = range(n_chunks)
    b_mid = [b[c * L + mid:c * L + mid + 1, :] for c in chunks]
    b_last = [b[c * L + last:c * L + last + 1, :] for c in chunks]
    scanned = {}
    run = jnp.zeros((1, dk), F32)
    for c in order:
        scanned[c] = run
        run = run + b_last[c]
    b_tot = run
    rows_of = lambda vals: jnp.concatenate([jnp.broadcast_to(x, (L, dk)) for x in vals], axis=0)

    mid_full = rows_of(b_mid)
    qe = q * jnp.exp2(b - mid_full)
    ki = k * jnp.exp2(mid_full - b)
    ke_loc = (ki * rows_of([jnp.exp2(b_last[c] - b_mid[c]) for c in chunks])).astype(BF16)
    chunk_rows = lambda a, c: a[c * L:(c + 1) * L, :]
    qx_parts, ke_parts = [], []
    for c_src in order[:-1]:
        ref = scanned[c_src] + b_last[c_src]
        later = [(c > c_src) if forward else (c < c_src) for c in chunks]
        qx_parts.append(jnp.concatenate(
            [(chunk_rows(qe, c) * jnp.exp2(b_mid[c] + scanned[c] - ref)).astype(BF16) if later[c]
             else jnp.zeros((L, dk), BF16) for c in chunks], axis=0))
        ke_parts.append(jnp.concatenate(
            [chunk_rows(ke_loc, c) if c == c_src else jnp.zeros((L, dk), BF16) for c in chunks], axis=0))
    cross = _dot_nt(jnp.concatenate(qx_parts, axis=1), jnp.concatenate(ke_parts, axis=1))
    p = jnp.where(diag_mask, _dot_nt(qe.astype(BF16), ki.astype(BF16)), cross)
    yield None
    vb = v.astype(BF16)
    st = st_ref[...]
    qb = qe * rows_of([jnp.exp2(b_mid[c] + scanned[c]) for c in chunks])
    o = _dot(p.astype(BF16), vb) + _dot_nt(qb.astype(BF16), st.astype(BF16))
    ke = ki * rows_of([jnp.exp2(b_tot - scanned[c] - b_mid[c]) for c in chunks])
    st_ref[...] = st * jnp.exp2(b_tot) + _dot_tn(vb, ke.astype(BF16))
    yield o


def _gla_kernel(qf_ref, kf_ref, vf_ref, zf_ref, qb_ref, kb_ref, vb_ref, zb_ref,
                upf_ref, biasf_ref, upb_ref, biasb_ref, of_ref, ob_ref, sf_ref, sb_ref,
                *, q_scale):
    @pl.when(pl.program_id(2) == 0)
    def _():
        sf_ref[...] = jnp.zeros_like(sf_ref)
        sb_ref[...] = jnp.zeros_like(sb_ref)

    def gate(z_ref, up_ref, bias_ref):
        z = z_ref[...]
        z = jnp.where(lax.broadcasted_iota(I32, z.shape, 1) < 2 * GLA_GATE_RANK, z, 0.0)
        pre = _dot(z.astype(BF16), up_ref[...]) + bias_ref[...]
        return _log_sigmoid(pre) * (math.log2(math.e) / GLA_GATE_NORMALIZER)

    g_f = gate(zf_ref, upf_ref, biasf_ref)
    g_b = gate(zb_ref, upb_ref, biasb_ref)

    dv, dk = sf_ref.shape[1:]
    blocks = []
    for hh in range(GLA_HEADS_PER_STEP):
        ck = slice(hh * dk, (hh + 1) * dk)
        cv = slice(hh * dv, (hh + 1) * dv)
        blocks.append((of_ref, cv, _gla_block(qf_ref[:, ck] * q_scale, kf_ref[:, ck], vf_ref[:, cv],
                                              g_f[:, ck], sf_ref.at[hh], True)))
        blocks.append((ob_ref, cv, _gla_block(qb_ref[:, ck] * q_scale, kb_ref[:, ck], vb_ref[:, cv],
                                              g_b[:, ck], sb_ref.at[hh], False)))
    for _ in range(GLA_STAGES - 1):
        for _, _, blk in blocks:
            next(blk)
    for out_ref, cv, blk in blocks:
        out_ref[:, cv] = next(blk)


def _gla(pm, z, upf, biasf, upb, biasb, batch, seq, dk, dv):
    n = pm.shape[0]
    nb = seq // GLA_BLOCK
    H = GLA_HEADS
    hs = GLA_HEADS_PER_STEP
    kw = H * dk
    fwd = lambda b, h, i: b * nb + i
    bwd = lambda b, h, i: b * nb + nb - 1 - i
    k_col = kw // (hs * dk)
    v_col = 2 * kw // (hs * dv)

    def specs(rowf):
        return [pl.BlockSpec((GLA_BLOCK, hs * dk), lambda b, h, i: (rowf(b, h, i), h)),
                pl.BlockSpec((GLA_BLOCK, hs * dk), lambda b, h, i: (rowf(b, h, i), k_col + h)),
                pl.BlockSpec((GLA_BLOCK, hs * dv), lambda b, h, i: (rowf(b, h, i), v_col + h)),
                pl.BlockSpec((GLA_BLOCK, LANES), lambda b, h, i: (rowf(b, h, i), 0))]

    w_specs = [pl.BlockSpec((LANES, hs * dk), lambda b, h, i: (0, h)),
               pl.BlockSpec((1, hs * dk), lambda b, h, i: (0, h))]
    out_shape = jax.ShapeDtypeStruct((n, H * dv), F32)
    return pl.pallas_call(
        functools.partial(_gla_kernel, q_scale=dk ** -0.5),
        grid=(batch, H // hs, nb),
        in_specs=specs(fwd) + specs(bwd) + w_specs + w_specs,
        out_specs=[pl.BlockSpec((GLA_BLOCK, hs * dv), lambda b, h, i: (fwd(b, h, i), h)),
                   pl.BlockSpec((GLA_BLOCK, hs * dv), lambda b, h, i: (bwd(b, h, i), h))],
        out_shape=[out_shape, out_shape],
        scratch_shapes=[pltpu.VMEM((hs, dv, dk), F32), pltpu.VMEM((hs, dv, dk), F32)],
        compiler_params=_params(("parallel", "parallel", "arbitrary"), 48),
        name="gla",
    )(pm, pm, pm, z, pm, pm, pm, z, upf, biasf, upb, biasb)


ATTN_TQ = 256
ATTN_TK = 256
ATTN_SUB = 64
ATTN_VROWS = 2 * DIFF_HD + 16
ATTN_VT_CHUNK = 512
ATTN_HEADS_PER_STEP = 2


def _diff_attn_kernel(q_ref, k_ref, v_ref, lam_ref, w_ref, o_ref, s_ref, vt_ref, *, lam_init, eps):
    seq = k_ref.shape[0]
    vd = 2 * DIFF_HD
    heads = range(ATTN_HEADS_PER_STEP)
    cols = [slice(h * vd, (h + 1) * vd) for h in heads]
    pad_row = lax.broadcasted_iota(I32, (ATTN_VROWS - vd, seq), 0)
    for h in heads:
        for r in range(0, seq, ATTN_VT_CHUNK):
            vt_ref[h, :vd, r:r + ATTN_VT_CHUNK] = v_ref[r:r + ATTN_VT_CHUNK, cols[h]].astype(F32).T.astype(BF16)
        vt_ref[h, vd:, :] = jnp.where(pad_row == 0, 1.0, 0.0).astype(BF16)

    lv = lam_ref[...]
    lam = (jnp.exp(jnp.sum(lv[0:1] * lv[1:2], axis=-1, keepdims=True))
           - jnp.exp(jnp.sum(lv[2:3] * lv[3:4], axis=-1, keepdims=True)) + lam_init)
    lane = lax.broadcasted_iota(I32, (ATTN_TQ, vd), 1)
    w_col = w_ref[...]

    n_tiles = seq // ATTN_TQ
    n_chunks = seq // ATTN_TK
    groups = ATTN_TK // 8

    def masked_q(h, tile, c):
        q = q_ref[pl.ds(pl.multiple_of(tile * ATTN_TQ, ATTN_TQ), ATTN_TQ), cols[h]]
        keep = (lane < DIFF_HD) if c == 0 else (lane >= DIFF_HD)
        return jnp.where(keep, q, jnp.zeros_like(q))

    def score_chunk(h, qc, c, j, mrun):
        st = _dot_nt(k_ref[j * ATTN_TK:(j + 1) * ATTN_TK, cols[h]], qc)
        s_ref[h, c, j * ATTN_TK:(j + 1) * ATTN_TK, :] = st
        return jnp.maximum(mrun, jnp.max(st.reshape(groups, 8, ATTN_TQ), axis=0))

    def value_chunk(h, c, j, m, acc):
        pieces = [jnp.exp2(s_ref[h, c, r:r + ATTN_SUB, :] - m).astype(BF16)
                  for r in range(j * ATTN_TK, (j + 1) * ATTN_TK, ATTN_SUB)]
        return acc + _dot(vt_ref[h, :, j * ATTN_TK:(j + 1) * ATTN_TK], jnp.concatenate(pieces, axis=0))

    def scores(h, tile, c):
        qc = masked_q(h, tile, c)
        mrun = jnp.full((8, ATTN_TQ), -jnp.inf, F32)
        for j in range(n_chunks):
            mrun = score_chunk(h, qc, c, j, mrun)
        return jnp.max(mrun, axis=0, keepdims=True)

    def overlapped(h, c_val, m, tile, c_score):
        qc = masked_q(h, tile, c_score)
        mrun = jnp.full((8, ATTN_TQ), -jnp.inf, F32)
        acc = jnp.zeros((ATTN_VROWS, ATTN_TQ), F32)
        for j in range(n_chunks):
            acc = value_chunk(h, c_val, j, m, acc)
        for j in range(n_chunks):
            mrun = score_chunk(h, qc, c_score, j, mrun)
        o = acc[:vd] / acc[vd:vd + 1]
        return o, jnp.max(mrun, axis=0, keepdims=True)

    def q_tile(i, m0):
        nxt = jnp.minimum(i + 1, n_tiles - 1)
        first = [overlapped(h, 0, m0[h], i, 1) for h in heads]
        second = [overlapped(h, 1, first[h][1], nxt, 0) for h in heads]
        for h in heads:
            o = first[h][0] - lam * second[h][0]
            ms = jnp.mean(o * o, axis=0, keepdims=True)
            y = (o * lax.rsqrt(ms + eps) * w_col) * (1.0 - lam_init)
            o_ref[pl.ds(pl.multiple_of(i * ATTN_TQ, ATTN_TQ), ATTN_TQ), cols[h]] = y.T
        return tuple(second[h][1] for h in heads)

    lax.fori_loop(0, n_tiles, q_tile, tuple(scores(h, 0, 0) for h in heads))


def _diff_attn(qk, v, lam_vecs, subln_w, batch, seq, heads, lam_init):
    n = qk.shape[0]
    vd = 2 * DIFF_HD
    hs = ATTN_HEADS_PER_STEP
    steps = heads // hs
    return pl.pallas_call(
        functools.partial(_diff_attn_kernel, lam_init=lam_init, eps=SUBLN_EPS),
        grid=(batch, steps),
        in_specs=[pl.BlockSpec((seq, hs * vd), lambda b, h: (b, h)),
                  pl.BlockSpec((seq, hs * vd), lambda b, h: (b, steps + h)),
                  pl.BlockSpec((seq, hs * vd), lambda b, h: (b, h)),
                  pl.BlockSpec((8, LANES), lambda b, h: (0, 0)),
                  pl.BlockSpec((vd, 1), lambda b, h: (0, 0))],
        out_specs=pl.BlockSpec((seq, hs * vd), lambda b, h: (b, h)),
        out_shape=jax.ShapeDtypeStruct((n, heads * vd), F32),
        scratch_shapes=[pltpu.VMEM((hs, 2, seq, ATTN_TQ), F32), pltpu.VMEM((hs, ATTN_VROWS, seq), BF16)],
        compiler_params=_params(("parallel", "parallel"), 48),
        name="diff_attn",
    )(qk, qk, v, lam_vecs, subln_w.reshape(vd, 1))


def _merge_kernel(of_ref, ob_ref, r_ref, yb_ref, ga_ref, gb_ref, x_ref, wout_ref, gnw_ref,
                  fnw_ref, wr_ref, x1_ref, h2_ref, aff_ref, *, dv, n_experts):
    o = of_ref[...] + ob_ref[...]
    segs = []
    for hh in range(o.shape[1] // dv):
        seg = o[:, hh * dv:(hh + 1) * dv]
        ms = jnp.mean(seg * seg, axis=-1, keepdims=True)
        segs.append(seg * lax.rsqrt(ms + NORM_EPS))
    on = jnp.concatenate(segs, axis=1) * gnw_ref[...]
    r = r_ref[...]
    ya = on * (r * jax.nn.sigmoid(r))
    merged = jax.nn.sigmoid(ga_ref[...]) * ya + jax.nn.sigmoid(gb_ref[...]) * yb_ref[...]
    x1 = x_ref[...] + _dot(merged.astype(BF16), wout_ref[...])
    x1_ref[...] = x1
    ms = jnp.mean(x1 * x1, axis=-1, keepdims=True)
    h2 = (x1 * lax.rsqrt(ms + NORM_EPS) * fnw_ref[...]).astype(BF16)
    h2_ref[...] = _pack_halves(h2)
    logits = _dot(h2, wr_ref[...])
    lane = lax.broadcasted_iota(I32, logits.shape, 1)
    logits = jnp.where(lane < n_experts, logits, -jnp.inf)
    e = jnp.exp(logits - jnp.max(logits, axis=-1, keepdims=True))
    aff_ref[...] = e / jnp.sum(e, axis=-1, keepdims=True)


def _merge(o_f, o_b, pm, yb, gates, x, wout, gnw, fnw, wr, dv, tm=256):
    n, d = x.shape
    r_col = (pm.shape[1] - d) // d
    row = lambda i: (i, 0)
    const = lambda i: (0, 0)
    return pl.pallas_call(
        functools.partial(_merge_kernel, dv=dv, n_experts=N_EXPERTS),
        grid=(n // tm,),
        in_specs=[pl.BlockSpec((tm, d), row), pl.BlockSpec((tm, d), row),
                  pl.BlockSpec((tm, d), lambda i: (i, r_col)),
                  pl.BlockSpec((tm, d), row),
                  pl.BlockSpec((tm, d), lambda i: (i, 0)), pl.BlockSpec((tm, d), lambda i: (i, 1)),
                  pl.BlockSpec((tm, d), row),
                  pl.BlockSpec((d, d), const, pipeline_mode=pl.Buffered(1)),
                  pl.BlockSpec((1, d), const),
                  pl.BlockSpec((1, d), const), pl.BlockSpec((d, LANES), const)],
        out_specs=[pl.BlockSpec((tm, d), row), pl.BlockSpec((tm, d // 2), row),
                   pl.BlockSpec((tm, LANES), row)],
        out_shape=[jax.ShapeDtypeStruct((n, d), F32), jax.ShapeDtypeStruct((n, d // 2), U32),
                   jax.ShapeDtypeStruct((n, LANES), F32)],
        compiler_params=_params(("parallel",), 56),
        name="merge_outproj",
    )(o_f, o_b, pm, yb, gates, gates, x, wout, gnw, fnw, wr)


TOPK_LANE_BLOCK = 256


def _exclusive_prefix_count(flags, strict_upper):
    e, t = flags.shape
    carry = jnp.zeros((e, 1), F32)
    parts = []
    for blk in range(t // TOPK_LANE_BLOCK):
        f = flags[:, blk * TOPK_LANE_BLOCK:(blk + 1) * TOPK_LANE_BLOCK]
        parts.append(_dot(f.astype(BF16), strict_upper) + carry)
        carry = carry + jnp.sum(f, axis=-1, keepdims=True)
    return jnp.concatenate(parts, axis=1)


def _topk_kernel(afft_ref, aff_ref, idx_ref, g_ref, posm_ref, *, cap):
    a = afft_ref[0]
    n_exp, t = a.shape
    bits = pltpu.bitcast(a, I32)

    def search(i, thr):
        cand = thr | jnp.left_shift(jnp.int32(1), 30 - i)
        cnt = jnp.sum(jnp.where(bits >= cand, 1.0, 0.0), axis=-1, keepdims=True)
        return jnp.where(cnt >= cap, cand, thr)

    thr = lax.fori_loop(0, 31, search, jnp.zeros((n_exp, 1), I32))
    gt = jnp.where(bits > thr, 1.0, 0.0)
    eq = jnp.where(bits == thr, 1.0, 0.0)
    need = cap - jnp.sum(gt, axis=-1, keepdims=True)

    r = lax.broadcasted_iota(I32, (TOPK_LANE_BLOCK, TOPK_LANE_BLOCK), 0)
    c = lax.broadcasted_iota(I32, (TOPK_LANE_BLOCK, TOPK_LANE_BLOCK), 1)
    strict_upper = jnp.where(r < c, 1.0, 0.0).astype(BF16)

    tie_rank = _exclusive_prefix_count(eq, strict_upper)
    sel = gt + eq * jnp.where(tie_rank < need, 1.0, 0.0)
    pos = _exclusive_prefix_count(sel, strict_upper)
    posm_ref[...] = jnp.where(sel > 0.5, pos, -1.0)

    av = aff_ref[0]
    hi, mid, lo = _split3(av)
    lane = lax.broadcasted_iota(I32, av.shape, 1)
    tok = lax.broadcasted_iota(I32, av.shape, 0)
    digits = jnp.where(lane == 0, (tok >> 6).astype(F32),
                       jnp.where(lane == 1, (tok & 63).astype(F32), 0.0))
    feat = (digits + pltpu.roll(hi.astype(F32), 2, 1) + pltpu.roll(mid.astype(F32), 2 + n_exp, 1)
            + pltpu.roll(lo.astype(F32), 2 + 2 * n_exp, 1)).astype(BF16)

    slot = lax.broadcasted_iota(I32, (cap, t), 0).astype(F32)
    out_lane = lax.broadcasted_iota(I32, (cap, LANES), 1)

    def per_expert(e, _):
        onehot = jnp.where(posm_ref[pl.ds(e, 1), :] == slot, 1.0, 0.0).astype(BF16)
        res = _dot(onehot, feat)
        idx_ref[0, e] = (res[:, 0:1] * 64.0 + res[:, 1:2]).astype(I32)
        mine = (out_lane == 2 + e) | (out_lane == 2 + n_exp + e) | (out_lane == 2 + 2 * n_exp + e)
        g_ref[0, e] = jnp.sum(jnp.where(mine, res, 0.0), axis=-1, keepdims=True)
        return 0

    lax.fori_loop(0, n_exp, per_expert, 0, unroll=2)


def _topk(afft, aff, cap):
    batch, n_exp, t = afft.shape
    return pl.pallas_call(
        functools.partial(_topk_kernel, cap=cap),
        grid=(batch,),
        in_specs=[pl.BlockSpec((1, n_exp, t), lambda b: (b, 0, 0)),
                  pl.BlockSpec((1, t, LANES), lambda b: (b, 0, 0))],
        out_specs=[pl.BlockSpec((1, n_exp, cap, 1), lambda b: (b, 0, 0, 0)),
                   pl.BlockSpec((1, n_exp, cap, 1), lambda b: (b, 0, 0, 0))],
        out_shape=[jax.ShapeDtypeStruct((batch, n_exp, cap, 1), I32),
                   jax.ShapeDtypeStruct((batch, n_exp, cap, 1), F32)],
        scratch_shapes=[pltpu.VMEM((n_exp, t), F32)],
        compiler_params=_params(("parallel",), 48),
        name="expert_topk",
    )(afft, aff)


def _dispatch_kernel(idx_ref, h_ref, o_ref, *, cap, n_exp):
    base = (pl.program_id(0) * n_exp + pl.program_id(1)) * cap

    def body(c, _):
        o_ref[pl.ds(c, 1), :] = h_ref[pl.ds(idx_ref[base + c], 1), :]
        return 0

    lax.fori_loop(0, cap, body, 0, unroll=8)


def _dispatch(idx_flat, hp, batch, seq, cap):
    width = hp.shape[1]
    return pl.pallas_call(
        functools.partial(_dispatch_kernel, cap=cap, n_exp=N_EXPERTS),
        grid_spec=pltpu.PrefetchScalarGridSpec(
            num_scalar_prefetch=1,
            grid=(batch, N_EXPERTS),
            in_specs=[pl.BlockSpec((seq, width), lambda b, e, idx: (b, 0))],
            out_specs=pl.BlockSpec((None, cap, width), lambda b, e, idx: (e, b, 0))),
        out_shape=jax.ShapeDtypeStruct((N_EXPERTS, batch * cap, width), hp.dtype),
        compiler_params=_params(("arbitrary", "arbitrary"), 48),
        name="dispatch",
    )(idx_flat, hp)


def _ffn_kernel(x_ref, wg_ref, wu_ref, wd_ref, g_ref, o_ref, xs_ref):
    f = pl.program_id(1)
    d = xs_ref.shape[1]

    @pl.when(f == 0)
    def _():
        lo, hi = _unpack_halves(x_ref[...])
        xs_ref[:, :d // 2] = lo
        xs_ref[:, d // 2:] = hi
        o_ref[...] = jnp.zeros_like(o_ref)

    x = xs_ref[...]
    a = _dot(x, wg_ref[...].astype(BF16))
    u = _dot(x, wu_ref[...].astype(BF16))
    hid = (a * jax.nn.sigmoid(a) * u).astype(BF16)
    o_ref[...] += _dot(hid, wd_ref[...].astype(BF16))

    @pl.when(f == pl.num_programs(1) - 1)
    def _():
        o_ref[...] = o_ref[...] * g_ref[...]


def _ffn(xe, wg, wu, wd, g, tf=256):
    n_exp, rows, half = xe.shape
    d = 2 * half
    dff = wg.shape[2]
    return pl.pallas_call(
        _ffn_kernel,
        grid=(n_exp, dff // tf),
        in_specs=[pl.BlockSpec((None, rows, half), lambda e, f: (e, 0, 0)),
                  pl.BlockSpec((None, d, tf), lambda e, f: (e, 0, f)),
                  pl.BlockSpec((None, d, tf), lambda e, f: (e, 0, f)),
                  pl.BlockSpec((None, tf, d), lambda e, f: (e, f, 0)),
                  pl.BlockSpec((None, rows, 1), lambda e, f: (e, 0, 0))],
        out_specs=pl.BlockSpec((None, rows, d), lambda e, f: (e, 0, 0)),
        out_shape=jax.ShapeDtypeStruct((n_exp, rows, d), F32),
        scratch_shapes=[pltpu.VMEM((rows, d), BF16)],
        compiler_params=_params(("parallel", "arbitrary"), 56),
        name="expert_ffn",
    )(xe, wg, wu, wd, g)


COMBINE_UNROLL = 8


def _combine_kernel(idx_ref, x1_ref, ye_ref, w_ref, o_ref, acc_ref, *, cap, n_exp):
    b = pl.program_id(0)
    s = pl.program_id(1)
    chunk = o_ref.shape[0]

    @pl.when(s == 0)
    def _():
        acc_ref[...] = jnp.zeros_like(acc_ref)

    @pl.when(s < n_exp)
    def _():
        rows = pl.ds(pl.multiple_of(s * chunk, chunk), chunk)
        acc_ref[rows, :] = acc_ref[rows, :] + x1_ref[...]
        base = (b * n_exp + s) * cap

        def group(i, _):
            c = i * COMBINE_UNROLL
            dst = [pl.ds(idx_ref[base + c + k], 1) for k in range(COMBINE_UNROLL)]
            sums = [acc_ref[dst[k], :] + ye_ref[pl.ds(c + k, 1), :] for k in range(COMBINE_UNROLL)]
            for k in range(COMBINE_UNROLL):
                acc_ref[dst[k], :] = sums[k]
            return 0

        lax.fori_loop(0, cap // COMBINE_UNROLL, group, 0)

    @pl.when(s >= n_exp)
    def _():
        x = acc_ref[pl.ds(pl.multiple_of((s - n_exp) * chunk, chunk), chunk), :]
        ms = jnp.mean(x * x, axis=-1, keepdims=True)
        o_ref[...] = x * lax.rsqrt(ms + NORM_EPS) * w_ref[...]


def _combine(idx_flat, x1, ye, w, batch, seq, cap):
    n, d = x1.shape
    n_exp = N_EXPERTS
    chunk = seq // n_exp
    assert cap % COMBINE_UNROLL == 0
    return pl.pallas_call(
        functools.partial(_combine_kernel, cap=cap, n_exp=n_exp),
        grid_spec=pltpu.PrefetchScalarGridSpec(
            num_scalar_prefetch=1,
            grid=(batch, 2 * n_exp),
            in_specs=[pl.BlockSpec((chunk, d), lambda b, s, idx: (b * n_exp + jnp.minimum(s, n_exp - 1), 0)),
                      pl.BlockSpec((None, cap, d), lambda b, s, idx: (jnp.minimum(s, n_exp - 1), b, 0)),
                      pl.BlockSpec((1, d), lambda b, s, idx: (0, 0))],
            out_specs=pl.BlockSpec((chunk, d), lambda b, s, idx: (b * n_exp + jnp.maximum(s - n_exp, 0), 0)),
            scratch_shapes=[pltpu.VMEM((seq, d), F32)]),
        out_shape=jax.ShapeDtypeStruct((n, d), F32),
        compiler_params=_params(("arbitrary", "arbitrary"), 56),
        name="combine_norm",
    )(idx_flat, x1, ye, w.reshape(1, d))


def _rope_tables(seq):
    pos = jnp.arange(seq, dtype=F32)
    inv_freq = ROPE_THETA ** (-jnp.arange(0, DIFF_HD, 2, dtype=F32) / DIFF_HD)
    ang = pos[:, None] * inv_freq[None, :]
    cos, sin = jnp.cos(ang), jnp.sin(ang)
    reps = LANES // (DIFF_HD // 2)
    sign = jnp.tile(jnp.concatenate([-jnp.ones((DIFF_HD // 2,), F32), jnp.ones((DIFF_HD // 2,), F32)]),
                    LANES // DIFF_HD)
    return jnp.tile(cos, (1, reps)), jnp.tile(sin, (1, reps)) * sign[None, :]


def kernel(x, norm_mix_w, w_in, gla_gate_up_fwd, gla_gate_bias_fwd, gla_gate_up_bwd, gla_gate_bias_bwd, gla_norm_w, diff_lambda_q1, diff_lambda_k1, diff_lambda_q2, diff_lambda_k2, diff_subln_w, w_out, norm_ffn_w, w_router, w_gate_e, w_up_e, w_down_e, norm_final_w):
    batch, seq, d = x.shape
    depth = w_in.shape[0]
    assert depth == 1, "the combine stage applies the final norm, so it must follow the only layer"
    n = batch * seq
    kw = gla_gate_up_fwd.shape[2]
    dk = kw // GLA_HEADS
    dv = d // GLA_HEADS
    rank = gla_gate_up_fwd.shape[1]
    diff_heads = d // (2 * DIFF_HD)
    cap = CAPACITY_FACTOR * seq // N_EXPERTS
    cos_t, sin_t = _rope_tables(seq)

    o_z = 2 * kw + 2 * d
    o_dq = o_z + 2 * rank
    o_dv = o_dq + 2 * d
    o_gate = o_dv + d

    xf = x.reshape(n, d)
    for l in range(depth):
        w = jnp.transpose(w_in[l])
        h = _rmsnorm(xf, norm_mix_w[l], NORM_EPS, BF16)
        pm = _proj(h, w, 0, o_z, F32)
        z = _proj(h, w, o_z, LANES, F32)
        qk = _proj(h, w, o_dq, 2 * d, BF16,
                   rope=(cos_t, sin_t, d, DIFF_HD ** -0.5 * math.log2(math.e), seq))
        vd = _proj(h, w, o_dv, d, BF16)
        gates = _proj(h, w, o_gate, 2 * d, F32)

        zero_rows = lambda a, before: jnp.pad(a, ((before, LANES - rank - before), (0, 0))).astype(BF16)
        o_f, o_b = _gla(pm, z, zero_rows(gla_gate_up_fwd[l], 0), gla_gate_bias_fwd[l].reshape(1, kw),
                        zero_rows(gla_gate_up_bwd[l], rank), gla_gate_bias_bwd[l].reshape(1, kw),
                        batch, seq, dk, dv)

        lam_init = 0.8 - 0.6 * math.exp(-0.3 * l)
        lam_vecs = jnp.pad(jnp.stack([diff_lambda_q1[l], diff_lambda_k1[l],
                                      diff_lambda_q2[l], diff_lambda_k2[l]]).astype(F32),
                           ((0, 4), (0, LANES - DIFF_HD)))
        yb = _diff_attn(qk, vd, lam_vecs, diff_subln_w[l], batch, seq, diff_heads, lam_init)

        wr = jnp.pad(w_router[l], ((0, 0), (0, LANES - N_EXPERTS))).astype(BF16)
        x1, h2, aff = _merge(o_f, o_b, pm, yb, gates, xf, w_out[l].astype(BF16),
                             jnp.tile(gla_norm_w[l], GLA_HEADS).reshape(1, d),
                             norm_ffn_w[l].reshape(1, d), wr, dv)

        aff3 = aff.reshape(batch, seq, LANES)
        afft = jnp.transpose(aff3[:, :, :N_EXPERTS], (0, 2, 1))
        idx, g = _topk(afft, aff3, cap)
        idx_flat = idx.reshape(batch * N_EXPERTS * cap)
        g_e = jnp.transpose(g, (1, 0, 2, 3)).reshape(N_EXPERTS, batch * cap, 1)

        xe = _dispatch(idx_flat, h2, batch, seq, cap)
        ye = _ffn(xe, w_gate_e[l], w_up_e[l], w_down_e[l], g_e)

        xf = _combine(idx_flat, x1, ye, norm_final_w, batch, seq, cap)
    return xf.reshape(batch, seq, d)
```

```python
import functools
import math

import jax
import jax.numpy as jnp
from jax import lax
from jax.experimental import pallas as pl
from jax.experimental.pallas import tpu as pltpu

F32 = jnp.float32
BF16 = jnp.bfloat16
I32 = jnp.int32
U32 = jnp.uint32

GLA_HEADS = 4
GLA_GATE_RANK = 16
GLA_GATE_NORMALIZER = 16.0
GLA_CHUNK = 64
DIFF_HD = 64
ROPE_THETA = 10000.0
N_EXPERTS = 16
CAPACITY_FACTOR = 2
NORM_EPS = 1e-6
SUBLN_EPS = 1e-5

LANES = 128
VMEM_PHYSICAL = 64 * 1024 * 1024


def _params(semantics, vmem_mb):
    return pltpu.CompilerParams(dimension_semantics=semantics,
                                vmem_limit_bytes=vmem_mb * 1024 * 1024)


def _dot(a, b):
    return jnp.dot(a, b, preferred_element_type=F32)


def _dot_nt(a, b):
    return lax.dot_general(a, b, (((1,), (1,)), ((), ())), preferred_element_type=F32)


def _dot_tn(a, b):
    return lax.dot_general(a, b, (((0,), (0,)), ((), ())), preferred_element_type=F32)


def _pack_halves(x):
    n = x.shape[1] // 2
    lo = pltpu.bitcast(x[:, :n].astype(F32), U32)
    hi = pltpu.bitcast(x[:, n:].astype(F32), U32)
    return (lo >> 16) | (hi & jnp.uint32(0xFFFF0000))


def _unpack_halves(p):
    lo = pltpu.bitcast(p << 16, F32).astype(BF16)
    hi = pltpu.bitcast(p & jnp.uint32(0xFFFF0000), F32).astype(BF16)
    return lo, hi


def _split3(x):
    hi = x.astype(BF16)
    r1 = x - hi.astype(F32)
    mid = r1.astype(BF16)
    lo = (r1 - mid.astype(F32)).astype(BF16)
    return hi, mid, lo


def _rmsnorm_kernel(x_ref, w_ref, o_ref, *, eps):
    x = x_ref[...]
    ms = jnp.mean(x * x, axis=-1, keepdims=True)
    o_ref[...] = (x * lax.rsqrt(ms + eps) * w_ref[...]).astype(o_ref.dtype)


def _rmsnorm(x, w, eps, out_dtype, tm=512):
    n, d = x.shape
    return pl.pallas_call(
        functools.partial(_rmsnorm_kernel, eps=eps),
        grid=(n // tm,),
        in_specs=[pl.BlockSpec((tm, d), lambda i: (i, 0)),
                  pl.BlockSpec((1, d), lambda i: (0, 0))],
        out_specs=pl.BlockSpec((tm, d), lambda i: (i, 0)),
        out_shape=jax.ShapeDtypeStruct((n, d), out_dtype),
        compiler_params=_params(("parallel",), 32),
        name="rmsnorm",
    )(x, w.reshape(1, d))


def _proj_kernel(*refs, shift, rope, n_q_tiles, q_scale):
    h_ref, wa_ref = refs[0], refs[1]
    pos = 2
    wb_ref = None
    if shift:
        wb_ref = refs[pos]
        pos += 1
    if rope:
        cos_ref, sin_ref = refs[pos], refs[pos + 1]
        pos += 2
    o_ref, w_scr = refs[pos], refs[pos + 1]
    tn = wa_ref.shape[0]

    @pl.when(pl.program_id(1) == 0)
    def _():
        if shift:
            w_scr[:tn - shift, :] = wa_ref[shift:, :].astype(BF16)
            w_scr[tn - shift:, :] = wb_ref[...].astype(BF16)
        else:
            w_scr[...] = wa_ref[...].astype(BF16)

    y = _dot_nt(h_ref[...], w_scr[...])
    if not rope:
        o_ref[...] = y.astype(o_ref.dtype)
        return
    tm = y.shape[0]
    scale = jnp.where(pl.program_id(0) < n_q_tiles, q_scale, 1.0).astype(F32)
    cos = cos_ref[...]
    sin = sin_ref[...]
    lane = lax.broadcasted_iota(I32, (tm, LANES), 1)
    first_half = (lane & (DIFF_HD - 1)) < (DIFF_HD // 2)
    for c in range(tn // LANES):
        yc = y[:, c * LANES:(c + 1) * LANES]
        sw = jnp.where(first_half,
                       pltpu.roll(yc, LANES - DIFF_HD // 2, 1),
                       pltpu.roll(yc, DIFF_HD // 2, 1))
        o_ref[:, c * LANES:(c + 1) * LANES] = ((yc * cos + sw * sin) * scale).astype(o_ref.dtype)


BF16_SUBLANES = 16


def _proj(h, wt, col0, width, out_dtype, tm=1024, tn=1024, rope=None):
    n, k = h.shape
    tn = min(tn, width)
    shift = col0 % LANES
    base = col0 - shift
    assert base % tn == 0 and width % tn == 0 and shift % BF16_SUBLANES == 0
    in_specs = [pl.BlockSpec((tm, k), lambda j, i: (i, 0)),
                pl.BlockSpec((tn, k), lambda j, i: (base // tn + j, 0))]
    args = [h, wt]
    if shift:
        assert (base + tn) % shift == 0
        in_specs.append(pl.BlockSpec((shift, k), lambda j, i: ((base + (j + 1) * tn) // shift, 0)))
        args.append(wt)
    n_q_tiles, q_scale = 0, 1.0
    if rope is not None:
        cos, sin, n_q_cols, q_scale, seq = rope
        n_q_tiles = n_q_cols // tn
        blocks_per_seq = seq // tm
        in_specs += [pl.BlockSpec((tm, LANES), lambda j, i: (i % blocks_per_seq, 0)),
                     pl.BlockSpec((tm, LANES), lambda j, i: (i % blocks_per_seq, 0))]
        args += [cos, sin]
    return pl.pallas_call(
        functools.partial(_proj_kernel, shift=shift, rope=rope is not None,
                          n_q_tiles=n_q_tiles, q_scale=q_scale),
        grid=(width // tn, n // tm),
        in_specs=in_specs,
        out_specs=pl.BlockSpec((tm, tn), lambda j, i: (i, j)),
        out_shape=jax.ShapeDtypeStruct((n, width), out_dtype),
        scratch_shapes=[pltpu.VMEM((tn, k), BF16)],
        compiler_params=_params(("parallel", "arbitrary"), 52),
        name="proj_rope" if rope is not None else "proj",
    )(*args)


GLA_BLOCK = 256
GLA_HEADS_PER_STEP = 4
GLA_STAGES = 3


def _log_sigmoid(x):
    return jnp.minimum(x, 0.0) - jnp.log1p(jnp.exp(-jnp.abs(x)))


def _gla_block(q, k, v, g, st_ref, forward):
    L = GLA_CHUNK
    n = GLA_BLOCK
    n_chunks = n // L
    row = lax.broadcasted_iota(I32, (n, n), 0)
    col = lax.broadcasted_iota(I32, (n, n), 1)
    rowc = row >> (L.bit_length() - 1)
    colc = col >> (L.bit_length() - 1)
    same = rowc == colc
    if forward:
        tri = jnp.where(same & (col <= row), 1.0, 0.0).astype(BF16)
        diag_mask = same & (col <= row)
        order = list(range(n_chunks))
        mid, last = L // 2 - 1, L - 1
    else:
        tri = jnp.where(same & (col >= row), 1.0, 0.0).astype(BF16)
        diag_mask = same & (col > row)
        order = list(range(n_chunks - 1, -1, -1))
        mid, last = L // 2, 0

    g_hi, g_mid, g_lo = _split3(g)
    b = _dot(tri, g_hi) + _dot(tri, g_mid) + _dot(tri, g_lo)
    yield None
    dk = b.shape[1]
    chunks = range(n_chunks)
    b_mid = [b[c * L + mid:c * L + mid + 1, :] for c in chunks]
    b_last = [b[c * L + last:c * L + last + 1, :] for c in chunks]
    scanned = {}
    run = jnp.zeros((1, dk), F32)
    for c in order:
        scanned[c] = run
        run = run + b_last[c]
    b_tot = run
    rows_of = lambda vals: jnp.concatenate([jnp.broadcast_to(x, (L, dk)) for x in vals], axis=0)

    mid_full = rows_of(b_mid)
    qe = q * jnp.exp2(b - mid_full)
    ki = k * jnp.exp2(mid_full - b)
    ke_loc = (ki * rows_of([jnp.exp2(b_last[c] - b_mid[c]) for c in chunks])).astype(BF16)
    chunk_rows = lambda a, c: a[c * L:(c + 1) * L, :]
    qx_parts, ke_parts = [], []
    for c_src in order[:-1]:
        ref = scanned[c_src] + b_last[c_src]
        later = [(c > c_src) if forward else (c < c_src) for c in chunks]
        qx_parts.append(jnp.concatenate(
            [(chunk_rows(qe, c) * jnp.exp2(b_mid[c] + scanned[c] - ref)).astype(BF16) if later[c]
             else jnp.zeros((L, dk), BF16) for c in chunks], axis=0))
        ke_parts.append(jnp.concatenate(
            [chunk_rows(ke_loc, c) if c == c_src else jnp.zeros((L, dk), BF16) for c in chunks], axis=0))
    cross = _dot_nt(jnp.concatenate(qx_parts, axis=1), jnp.concatenate(ke_parts, axis=1))
    p = jnp.where(diag_mask, _dot_nt(qe.astype(BF16), ki.astype(BF16)), cross)
    yield None
    vb = v.astype(BF16)
    st = st_ref[...]
    qb = qe * rows_of([jnp.exp2(b_mid[c] + scanned[c]) for c in chunks])
    o = _dot(p.astype(BF16), vb) + _dot_nt(qb.astype(BF16), st.astype(BF16))
    ke = ki * rows_of([jnp.exp2(b_tot - scanned[c] - b_mid[c]) for c in chunks])
    st_ref[...] = st * jnp.exp2(b_tot) + _dot_tn(vb, ke.astype(BF16))
    yield o


def _gla_kernel(qf_ref, kf_ref, vf_ref, zf_ref, qb_ref, kb_ref, vb_ref, zb_ref,
                upf_ref, biasf_ref, upb_ref, biasb_ref, of_ref, ob_ref, sf_ref, sb_ref,
                *, q_scale):
    @pl.when(pl.program_id(2) == 0)
    def _():
        sf_ref[...] = jnp.zeros_like(sf_ref)
        sb_ref[...] = jnp.zeros_like(sb_ref)

    def gate(z_ref, up_ref, bias_ref):
        z = z_ref[...]
        z = jnp.where(lax.broadcasted_iota(I32, z.shape, 1) < 2 * GLA_GATE_RANK, z, 0.0)
        pre = _dot(z.astype(BF16), up_ref[...]) + bias_ref[...]
        return _log_sigmoid(pre) * (math.log2(math.e) / GLA_GATE_NORMALIZER)

    g_f = gate(zf_ref, upf_ref, biasf_ref)
    g_b = gate(zb_ref, upb_ref, biasb_ref)

    dv, dk = sf_ref.shape[1:]
    blocks = []
    for hh in range(GLA_HEADS_PER_STEP):
        ck = slice(hh * dk, (hh + 1) * dk)
        cv = slice(hh * dv, (hh + 1) * dv)
        blocks.append((of_ref, cv, _gla_block(qf_ref[:, ck] * q_scale, kf_ref[:, ck], vf_ref[:, cv],
                                              g_f[:, ck], sf_ref.at[hh], True)))
        blocks.append((ob_ref, cv, _gla_block(qb_ref[:, ck] * q_scale, kb_ref[:, ck], vb_ref[:, cv],
                                              g_b[:, ck], sb_ref.at[hh], False)))
    for _ in range(GLA_STAGES - 1):
        for _, _, blk in blocks:
            next(blk)
    for out_ref, cv, blk in blocks:
        out_ref[:, cv] = next(blk)


def _gla(pm, z, upf, biasf, upb, biasb, batch, seq, dk, dv):
    n = pm.shape[0]
    nb = seq // GLA_BLOCK
    H = GLA_HEADS
    hs = GLA_HEADS_PER_STEP
    kw = H * dk
    fwd = lambda b, h, i: b * nb + i
    bwd = lambda b, h, i: b * nb + nb - 1 - i
    k_col = kw // (hs * dk)
    v_col = 2 * kw // (hs * dv)

    def specs(rowf):
        return [pl.BlockSpec((GLA_BLOCK, hs * dk), lambda b, h, i: (rowf(b, h, i), h)),
                pl.BlockSpec((GLA_BLOCK, hs * dk), lambda b, h, i: (rowf(b, h, i), k_col + h)),
                pl.BlockSpec((GLA_BLOCK, hs * dv), lambda b, h, i: (rowf(b, h, i), v_col + h)),
                pl.BlockSpec((GLA_BLOCK, LANES), lambda b, h, i: (rowf(b, h, i), 0))]

    w_specs = [pl.BlockSpec((LANES, hs * dk), lambda b, h, i: (0, h)),
               pl.BlockSpec((1, hs * dk), lambda b, h, i: (0, h))]
    out_shape = jax.ShapeDtypeStruct((n, H * dv), F32)
    return pl.pallas_call(
        functools.partial(_gla_kernel, q_scale=dk ** -0.5),
        grid=(batch, H // hs, nb),
        in_specs=specs(fwd) + specs(bwd) + w_specs + w_specs,
        out_specs=[pl.BlockSpec((GLA_BLOCK, hs * dv), lambda b, h, i: (fwd(b, h, i), h)),
                   pl.BlockSpec((GLA_BLOCK, hs * dv), lambda b, h, i: (bwd(b, h, i), h))],
        out_shape=[out_shape, out_shape],
        scratch_shapes=[pltpu.VMEM((hs, dv, dk), F32), pltpu.VMEM((hs, dv, dk), F32)],
        compiler_params=_params(("parallel", "parallel", "arbitrary"), 48),
        name="gla",
    )(pm, pm, pm, z, pm, pm, pm, z, upf, biasf, upb, biasb)


ATTN_TQ = 256
ATTN_TK = 256
ATTN_SUB = 64
ATTN_VROWS = 2 * DIFF_HD + 16
ATTN_VT_CHUNK = 512
ATTN_HEADS_PER_STEP = 2


def _diff_attn_kernel(q_ref, k_ref, v_ref, lam_ref, w_ref, o_ref, s_ref, vt_ref, *, lam_init, eps):
    seq = k_ref.shape[0]
    vd = 2 * DIFF_HD
    heads = range(ATTN_HEADS_PER_STEP)
    cols = [slice(h * vd, (h + 1) * vd) for h in heads]
    pad_row = lax.broadcasted_iota(I32, (ATTN_VROWS - vd, seq), 0)
    for h in heads:
        for r in range(0, seq, ATTN_VT_CHUNK):
            vt_ref[h, :vd, r:r + ATTN_VT_CHUNK] = v_ref[r:r + ATTN_VT_CHUNK, cols[h]].astype(F32).T.astype(BF16)
        vt_ref[h, vd:, :] = jnp.where(pad_row == 0, 1.0, 0.0).astype(BF16)

    lv = lam_ref[...]
    lam = (jnp.exp(jnp.sum(lv[0:1] * lv[1:2], axis=-1, keepdims=True))
           - jnp.exp(jnp.sum(lv[2:3] * lv[3:4], axis=-1, keepdims=True)) + lam_init)
    lane = lax.broadcasted_iota(I32, (ATTN_TQ, vd), 1)
    w_col = w_ref[...]

    n_tiles = seq // ATTN_TQ
    n_chunks = seq // ATTN_TK
    groups = ATTN_TK // 8

    def masked_q(h, tile, c):
        q = q_ref[pl.ds(pl.multiple_of(tile * ATTN_TQ, ATTN_TQ), ATTN_TQ), cols[h]]
        keep = (lane < DIFF_HD) if c == 0 else (lane >= DIFF_HD)
        return jnp.where(keep, q, jnp.zeros_like(q))

    def score_chunk(h, qc, c, j, mrun):
        st = _dot_nt(k_ref[j * ATTN_TK:(j + 1) * ATTN_TK, cols[h]], qc)
        s_ref[h, c, j * ATTN_TK:(j + 1) * ATTN_TK, :] = st
        return jnp.maximum(mrun, jnp.max(st.reshape(groups, 8, ATTN_TQ), axis=0))

    def value_chunk(h, c, j, m, acc):
        pieces = [jnp.exp2(s_ref[h, c, r:r + ATTN_SUB, :] - m).astype(BF16)
                  for r in range(j * ATTN_TK, (j + 1) * ATTN_TK, ATTN_SUB)]
        return acc + _dot(vt_ref[h, :, j * ATTN_TK:(j + 1) * ATTN_TK], jnp.concatenate(pieces, axis=0))

    def scores(h, tile, c):
        qc = masked_q(h, tile, c)
        mrun = jnp.full((8, ATTN_TQ), -jnp.inf, F32)
        for j in range(n_chunks):
            mrun = score_chunk(h, qc, c, j, mrun)
        return jnp.max(mrun, axis=0, keepdims=True)

    def overlapped(h, c_val, m, tile, c_score):
        qc = masked_q(h, tile, c_score)
        mrun = jnp.full((8, ATTN_TQ), -jnp.inf, F32)
        acc = jnp.zeros((ATTN_VROWS, ATTN_TQ), F32)
        for j in range(n_chunks):
            acc = value_chunk(h, c_val, j, m, acc)
        for j in range(n_chunks):
            mrun = score_chunk(h, qc, c_score, j, mrun)
        o = acc[:vd] / acc[vd:vd + 1]
        return o, jnp.max(mrun, axis=0, keepdims=True)

    def q_tile(i, m0):
        nxt = jnp.minimum(i + 1, n_tiles - 1)
        first = [overlapped(h, 0, m0[h], i, 1) for h in heads]
        second = [overlapped(h, 1, first[h][1], nxt, 0) for h in heads]
        for h in heads:
            o = first[h][0] - lam * second[h][0]
            ms = jnp.mean(o * o, axis=0, keepdims=True)
            y = (o * lax.rsqrt(ms + eps) * w_col) * (1.0 - lam_init)
            o_ref[pl.ds(pl.multiple_of(i * ATTN_TQ, ATTN_TQ), ATTN_TQ), cols[h]] = y.T
        return tuple(second[h][1] for h in heads)

    lax.fori_loop(0, n_tiles, q_tile, tuple(scores(h, 0, 0) for h in heads))


def _diff_attn(qk, v, lam_vecs, subln_w, batch, seq, heads, lam_init):
    n = qk.shape[0]
    vd = 2 * DIFF_HD
    hs = ATTN_HEADS_PER_STEP
    steps = heads // hs
    return pl.pallas_call(
        functools.partial(_diff_attn_kernel, lam_init=lam_init, eps=SUBLN_EPS),
        grid=(batch, steps),
        in_specs=[pl.BlockSpec((seq, hs * vd), lambda b, h: (b, h)),
                  pl.BlockSpec((seq, hs * vd), lambda b, h: (b, steps + h)),
                  pl.BlockSpec((seq, hs * vd), lambda b, h: (b, h)),
                  pl.BlockSpec((8, LANES), lambda b, h: (0, 0)),
                  pl.BlockSpec((vd, 1), lambda b, h: (0, 0))],
        out_specs=pl.BlockSpec((seq, hs * vd), lambda b, h: (b, h)),
        out_shape=jax.ShapeDtypeStruct((n, heads * vd), F32),
        scratch_shapes=[pltpu.VMEM((hs, 2, seq, ATTN_TQ), F32), pltpu.VMEM((hs, ATTN_VROWS, seq), BF16)],
        compiler_params=_params(("parallel", "parallel"), 48),
        name="diff_attn",
    )(qk, qk, v, lam_vecs, subln_w.reshape(vd, 1))


def _merge_kernel(of_ref, ob_ref, r_ref, yb_ref, ga_ref, gb_ref, x_ref, wout_ref, gnw_ref,
                  fnw_ref, wr_ref, x1_ref, h2_ref, aff_ref, *, dv, n_experts):
    o = of_ref[...] + ob_ref[...]
    segs = []
    for hh in range(o.shape[1] // dv):
        seg = o[:, hh * dv:(hh + 1) * dv]
        ms = jnp.mean(seg * seg, axis=-1, keepdims=True)
        segs.append(seg * lax.rsqrt(ms + NORM_EPS))
    on = jnp.concatenate(segs, axis=1) * gnw_ref[...]
    r = r_ref[...]
    ya = on * (r * jax.nn.sigmoid(r))
    merged = jax.nn.sigmoid(ga_ref[...]) * ya + jax.nn.sigmoid(gb_ref[...]) * yb_ref[...]
    x1 = x_ref[...] + _dot(merged.astype(BF16), wout_ref[...])
    x1_ref[...] = x1
    ms = jnp.mean(x1 * x1, axis=-1, keepdims=True)
    h2 = (x1 * lax.rsqrt(ms + NORM_EPS) * fnw_ref[...]).astype(BF16)
    h2_ref[...] = _pack_halves(h2)
    logits = _dot(h2, wr_ref[...])
    lane = lax.broadcasted_iota(I32, logits.shape, 1)
    logits = jnp.where(lane < n_experts, logits, -jnp.inf)
    e = jnp.exp(logits - jnp.max(logits, axis=-1, keepdims=True))
    aff_ref[...] = e / jnp.sum(e, axis=-1, keepdims=True)


def _merge(o_f, o_b, pm, yb, gates, x, wout, gnw, fnw, wr, dv, tm=256):
    n, d = x.shape
    r_col = (pm.shape[1] - d) // d
    row = lambda i: (i, 0)
    const = lambda i: (0, 0)
    return pl.pallas_call(
        functools.partial(_merge_kernel, dv=dv, n_experts=N_EXPERTS),
        grid=(n // tm,),
        in_specs=[pl.BlockSpec((tm, d), row), pl.BlockSpec((tm, d), row),
                  pl.BlockSpec((tm, d), lambda i: (i, r_col)),
                  pl.BlockSpec((tm, d), row),
                  pl.BlockSpec((tm, d), lambda i: (i, 0)), pl.BlockSpec((tm, d), lambda i: (i, 1)),
                  pl.BlockSpec((tm, d), row),
                  pl.BlockSpec((d, d), const, pipeline_mode=pl.Buffered(1)),
                  pl.BlockSpec((1, d), const),
                  pl.BlockSpec((1, d), const), pl.BlockSpec((d, LANES), const)],
        out_specs=[pl.BlockSpec((tm, d), row), pl.BlockSpec((tm, d // 2), row),
                   pl.BlockSpec((tm, LANES), row)],
        out_shape=[jax.ShapeDtypeStruct((n, d), F32), jax.ShapeDtypeStruct((n, d // 2), U32),
                   jax.ShapeDtypeStruct((n, LANES), F32)],
        compiler_params=_params(("parallel",), 56),
        name="merge_outproj",
    )(o_f, o_b, pm, yb, gates, gates, x, wout, gnw, fnw, wr)


TOPK_LANE_BLOCK = 256


def _exclusive_prefix_count(flags, strict_upper):
    e, t = flags.shape
    carry = jnp.zeros((e, 1), F32)
    parts = []
    for blk in range(t // TOPK_LANE_BLOCK):
        f = flags[:, blk * TOPK_LANE_BLOCK:(blk + 1) * TOPK_LANE_BLOCK]
        parts.append(_dot(f.astype(BF16), strict_upper) + carry)
        carry = carry + jnp.sum(f, axis=-1, keepdims=True)
    return jnp.concatenate(parts, axis=1)


def _topk_kernel(aff_ref, idx_ref, g_ref, posm_ref, *, cap, n_exp):
    av = aff_ref[0]
    t = av.shape[0]
    a = av.T[:n_exp, :]
    bits = pltpu.bitcast(a, I32)

    def search(i, thr):
        cand = thr | jnp.left_shift(jnp.int32(1), 30 - i)
        cnt = jnp.sum(jnp.where(bits >= cand, 1.0, 0.0), axis=-1, keepdims=True)
        return jnp.where(cnt >= cap, cand, thr)

    thr = lax.fori_loop(0, 31, search, jnp.zeros((n_exp, 1), I32))
    gt = jnp.where(bits > thr, 1.0, 0.0)
    eq = jnp.where(bits == thr, 1.0, 0.0)
    need = cap - jnp.sum(gt, axis=-1, keepdims=True)

    r = lax.broadcasted_iota(I32, (TOPK_LANE_BLOCK, TOPK_LANE_BLOCK), 0)
    c = lax.broadcasted_iota(I32, (TOPK_LANE_BLOCK, TOPK_LANE_BLOCK), 1)
    strict_upper = jnp.where(r < c, 1.0, 0.0).astype(BF16)

    tie_rank = _exclusive_prefix_count(eq, strict_upper)
    sel = gt + eq * jnp.where(tie_rank < need, 1.0, 0.0)
    pos = _exclusive_prefix_count(sel, strict_upper)
    posm_ref[...] = jnp.where(sel > 0.5, pos, -1.0)

    hi, mid, lo = _split3(av)
    lane = lax.broadcasted_iota(I32, av.shape, 1)
    tok = lax.broadcasted_iota(I32, av.shape, 0)
    digits = jnp.where(lane == 0, (tok >> 6).astype(F32),
                       jnp.where(lane == 1, (tok & 63).astype(F32), 0.0))
    feat = (digits + pltpu.roll(hi.astype(F32), 2, 1) + pltpu.roll(mid.astype(F32), 2 + n_exp, 1)
            + pltpu.roll(lo.astype(F32), 2 + 2 * n_exp, 1)).astype(BF16)

    slot = lax.broadcasted_iota(I32, (cap, t), 0).astype(F32)
    out_lane = lax.broadcasted_iota(I32, (cap, LANES), 1)

    def per_expert(e, _):
        onehot = jnp.where(posm_ref[pl.ds(e, 1), :] == slot, 1.0, 0.0).astype(BF16)
        res = _dot(onehot, feat)
        idx_ref[0, e] = (res[:, 0:1] * 64.0 + res[:, 1:2]).astype(I32)
        mine = (out_lane == 2 + e) | (out_lane == 2 + n_exp + e) | (out_lane == 2 + 2 * n_exp + e)
        g_ref[0, e] = jnp.sum(jnp.where(mine, res, 0.0), axis=-1, keepdims=True)
        return 0

    lax.fori_loop(0, n_exp, per_expert, 0, unroll=2)


def _topk(aff, cap, n_exp):
    batch, t, _ = aff.shape
    return pl.pallas_call(
        functools.partial(_topk_kernel, cap=cap, n_exp=n_exp),
        grid=(batch,),
        in_specs=[pl.BlockSpec((1, t, LANES), lambda b: (b, 0, 0))],
        out_specs=[pl.BlockSpec((1, n_exp, cap, 1), lambda b: (b, 0, 0, 0)),
                   pl.BlockSpec((1, n_exp, cap, 1), lambda b: (b, 0, 0, 0))],
        out_shape=[jax.ShapeDtypeStruct((batch, n_exp, cap, 1), I32),
                   jax.ShapeDtypeStruct((batch, n_exp, cap, 1), F32)],
        scratch_shapes=[pltpu.VMEM((n_exp, t), F32)],
        compiler_params=_params(("parallel",), 48),
        name="expert_topk",
    )(aff)


def _dispatch_kernel(idx_ref, h_ref, o_ref, *, cap, n_exp):
    base = (pl.program_id(0) * n_exp + pl.program_id(1)) * cap

    def body(c, _):
        o_ref[pl.ds(c, 1), :] = h_ref[pl.ds(idx_ref[base + c], 1), :]
        return 0

    lax.fori_loop(0, cap, body, 0, unroll=8)


def _dispatch(idx_flat, hp, batch, seq, cap):
    width = hp.shape[1]
    return pl.pallas_call(
        functools.partial(_dispatch_kernel, cap=cap, n_exp=N_EXPERTS),
        grid_spec=pltpu.PrefetchScalarGridSpec(
            num_scalar_prefetch=1,
            grid=(batch, N_EXPERTS),
            in_specs=[pl.BlockSpec((seq, width), lambda b, e, idx: (b, 0))],
            out_specs=pl.BlockSpec((None, cap, width), lambda b, e, idx: (e, b, 0))),
        out_shape=jax.ShapeDtypeStruct((N_EXPERTS, batch * cap, width), hp.dtype),
        compiler_params=_params(("arbitrary", "arbitrary"), 48),
        name="dispatch",
    )(idx_flat, hp)


def _ffn_kernel(x_ref, wg_ref, wu_ref, wd_ref, g_ref, o_ref, xs_ref):
    f = pl.program_id(1)
    d = xs_ref.shape[1]

    @pl.when(f == 0)
    def _():
        lo, hi = _unpack_halves(x_ref[...])
        xs_ref[:, :d // 2] = lo
        xs_ref[:, d // 2:] = hi
        o_ref[...] = jnp.zeros_like(o_ref)

    x = xs_ref[...]
    a = _dot(x, wg_ref[...].astype(BF16))
    u = _dot(x, wu_ref[...].astype(BF16))
    hid = (a * jax.nn.sigmoid(a) * u).astype(BF16)
    o_ref[...] += _dot(hid, wd_ref[...].astype(BF16))

    @pl.when(f == pl.num_programs(1) - 1)
    def _():
        o_ref[...] = o_ref[...] * g_ref[...].reshape(o_ref.shape[0], 1)


def _ffn(xe, wg, wu, wd, g, tf=256):
    n_exp, rows, half = xe.shape
    batch, _, cap, _ = g.shape
    d = 2 * half
    dff = wg.shape[2]
    return pl.pallas_call(
        _ffn_kernel,
        grid=(n_exp, dff // tf),
        in_specs=[pl.BlockSpec((None, rows, half), lambda e, f: (e, 0, 0)),
                  pl.BlockSpec((None, d, tf), lambda e, f: (e, 0, f)),
                  pl.BlockSpec((None, d, tf), lambda e, f: (e, 0, f)),
                  pl.BlockSpec((None, tf, d), lambda e, f: (e, f, 0)),
                  pl.BlockSpec((batch, None, cap, 1), lambda e, f: (0, e, 0, 0))],
        out_specs=pl.BlockSpec((None, rows, d), lambda e, f: (e, 0, 0)),
        out_shape=jax.ShapeDtypeStruct((n_exp, rows, d), F32),
        scratch_shapes=[pltpu.VMEM((rows, d), BF16)],
        compiler_params=_params(("parallel", "arbitrary"), 56),
        name="expert_ffn",
    )(xe, wg, wu, wd, g)


COMBINE_UNROLL = 8


def _combine_kernel(idx_ref, x1_ref, ye_ref, w_ref, o_ref, acc_ref, *, cap, n_exp):
    b = pl.program_id(0)
    s = pl.program_id(1)
    chunk = o_ref.shape[0]

    @pl.when(s == 0)
    def _():
        acc_ref[...] = jnp.zeros_like(acc_ref)

    @pl.when(s < n_exp)
    def _():
        rows = pl.ds(pl.multiple_of(s * chunk, chunk), chunk)
        acc_ref[rows, :] = acc_ref[rows, :] + x1_ref[...]
        base = (b * n_exp + s) * cap

        def group(i, _):
            c = i * COMBINE_UNROLL
            dst = [pl.ds(idx_ref[base + c + k], 1) for k in range(COMBINE_UNROLL)]
            sums = [acc_ref[dst[k], :] + ye_ref[pl.ds(c + k, 1), :] for k in range(COMBINE_UNROLL)]
            for k in range(COMBINE_UNROLL):
                acc_ref[dst[k], :] = sums[k]
            return 0

        lax.fori_loop(0, cap // COMBINE_UNROLL, group, 0)

    @pl.when(s >= n_exp)
    def _():
        x = acc_ref[pl.ds(pl.multiple_of((s - n_exp) * chunk, chunk), chunk), :]
        ms = jnp.mean(x * x, axis=-1, keepdims=True)
        o_ref[...] = x * lax.rsqrt(ms + NORM_EPS) * w_ref[...]


def _combine(idx_flat, x1, ye, w, batch, seq, cap):
    n, d = x1.shape
    n_exp = N_EXPERTS
    chunk = seq // n_exp
    assert cap % COMBINE_UNROLL == 0
    return pl.pallas_call(
        functools.partial(_combine_kernel, cap=cap, n_exp=n_exp),
        grid_spec=pltpu.PrefetchScalarGridSpec(
            num_scalar_prefetch=1,
            grid=(batch, 2 * n_exp),
            in_specs=[pl.BlockSpec((chunk, d), lambda b, s, idx: (b * n_exp + jnp.minimum(s, n_exp - 1), 0)),
                      pl.BlockSpec((None, cap, d), lambda b, s, idx: (jnp.minimum(s, n_exp - 1), b, 0)),
                      pl.BlockSpec((1, d), lambda b, s, idx: (0, 0))],
            out_specs=pl.BlockSpec((chunk, d), lambda b, s, idx: (b * n_exp + jnp.maximum(s - n_exp, 0), 0)),
            scratch_shapes=[pltpu.VMEM((seq, d), F32)]),
        out_shape=jax.ShapeDtypeStruct((n, d), F32),
        compiler_params=_params(("arbitrary", "arbitrary"), 56),
        name="combine_norm",
    )(idx_flat, x1, ye, w.reshape(1, d))


def _rope_tables(seq):
    pos = jnp.arange(seq, dtype=F32)
    inv_freq = ROPE_THETA ** (-jnp.arange(0, DIFF_HD, 2, dtype=F32) / DIFF_HD)
    ang = pos[:, None] * inv_freq[None, :]
    cos, sin = jnp.cos(ang), jnp.sin(ang)
    reps = LANES // (DIFF_HD // 2)
    sign = jnp.tile(jnp.concatenate([-jnp.ones((DIFF_HD // 2,), F32), jnp.ones((DIFF_HD // 2,), F32)]),
                    LANES // DIFF_HD)
    return jnp.tile(cos, (1, reps)), jnp.tile(sin, (1, reps)) * sign[None, :]


def kernel(x, norm_mix_w, w_in, gla_gate_up_fwd, gla_gate_bias_fwd, gla_gate_up_bwd, gla_gate_bias_bwd, gla_norm_w, diff_lambda_q1, diff_lambda_k1, diff_lambda_q2, diff_lambda_k2, diff_subln_w, w_out, norm_ffn_w, w_router, w_gate_e, w_up_e, w_down_e, norm_final_w):
    batch, seq, d = x.shape
    depth = w_in.shape[0]
    assert depth == 1, "the combine stage applies the final norm, so it must follow the only layer"
    n = batch * seq
    kw = gla_gate_up_fwd.shape[2]
    dk = kw // GLA_HEADS
    dv = d // GLA_HEADS
    rank = gla_gate_up_fwd.shape[1]
    diff_heads = d // (2 * DIFF_HD)
    cap = CAPACITY_FACTOR * seq // N_EXPERTS
    cos_t, sin_t = _rope_tables(seq)

    o_z = 2 * kw + 2 * d
    o_dq = o_z + 2 * rank
    o_dv = o_dq + 2 * d
    o_gate = o_dv + d

    xf = x.reshape(n, d)
    for l in range(depth):
        w = jnp.transpose(w_in[l])
        h = _rmsnorm(xf, norm_mix_w[l], NORM_EPS, BF16)
        pm = _proj(h, w, 0, o_z, F32)
        z = _proj(h, w, o_z, LANES, F32)
        qk = _proj(h, w, o_dq, 2 * d, BF16,
                   rope=(cos_t, sin_t, d, DIFF_HD ** -0.5 * math.log2(math.e), seq))
        vd = _proj(h, w, o_dv, d, BF16)
        gates = _proj(h, w, o_gate, 2 * d, F32)

        zero_rows = lambda a, before: jnp.pad(a, ((before, LANES - rank - before), (0, 0))).astype(BF16)
        o_f, o_b = _gla(pm, z, zero_rows(gla_gate_up_fwd[l], 0), gla_gate_bias_fwd[l].reshape(1, kw),
                        zero_rows(gla_gate_up_bwd[l], rank), gla_gate_bias_bwd[l].reshape(1, kw),
                        batch, seq, dk, dv)

        lam_init = 0.8 - 0.6 * math.exp(-0.3 * l)
        lam_vecs = jnp.pad(jnp.stack([diff_lambda_q1[l], diff_lambda_k1[l],
                                      diff_lambda_q2[l], diff_lambda_k2[l]]).astype(F32),
                           ((0, 4), (0, LANES - DIFF_HD)))
        yb = _diff_attn(qk, vd, lam_vecs, diff_subln_w[l], batch, seq, diff_heads, lam_init)

        wr = jnp.pad(w_router[l], ((0, 0), (0, LANES - N_EXPERTS))).astype(BF16)
        x1, h2, aff = _merge(o_f, o_b, pm, yb, gates, xf, w_out[l].astype(BF16),
                             jnp.tile(gla_norm_w[l], GLA_HEADS).reshape(1, d),
                             norm_ffn_w[l].reshape(1, d), wr, dv)

        idx, g = _topk(aff.reshape(batch, seq, LANES), cap, N_EXPERTS)
        idx_flat = idx.reshape(batch * N_EXPERTS * cap)

        xe = _dispatch(idx_flat, h2, batch, seq, cap)
        ye = _ffn(xe, w_gate_e[l], w_up_e[l], w_down_e[l], g)

        xf = _combine(idx_flat, x1, ye, norm_final_w, batch, seq, cap)
    return xf.reshape(batch, seq, d)
```

```python
import functools
import math

import jax
import jax.numpy as jnp
from jax import lax
from jax.experimental import pallas as pl
from jax.experimental.pallas import tpu as pltpu

F32 = jnp.float32
BF16 = jnp.bfloat16
I32 = jnp.int32
U32 = jnp.uint32

GLA_HEADS = 4
GLA_GATE_RANK = 16
GLA_GATE_NORMALIZER = 16.0
GLA_CHUNK = 64
DIFF_HD = 64
ROPE_THETA = 10000.0
N_EXPERTS = 16
CAPACITY_FACTOR = 2
NORM_EPS = 1e-6
SUBLN_EPS = 1e-5

LANES = 128
VMEM_PHYSICAL = 64 * 1024 * 1024


def _params(semantics, vmem_mb):
    return pltpu.CompilerParams(dimension_semantics=semantics,
                                vmem_limit_bytes=vmem_mb * 1024 * 1024)


def _dot(a, b):
    return jnp.dot(a, b, preferred_element_type=F32)


def _dot_nt(a, b):
    return lax.dot_general(a, b, (((1,), (1,)), ((), ())), preferred_element_type=F32)


def _dot_tn(a, b):
    return lax.dot_general(a, b, (((0,), (0,)), ((), ())), preferred_element_type=F32)


def _pack_halves(x):
    n = x.shape[1] // 2
    lo = pltpu.bitcast(x[:, :n].astype(F32), U32)
    hi = pltpu.bitcast(x[:, n:].astype(F32), U32)
    return (lo >> 16) | (hi & jnp.uint32(0xFFFF0000))


def _unpack_halves(p):
    lo = pltpu.bitcast(p << 16, F32).astype(BF16)
    hi = pltpu.bitcast(p & jnp.uint32(0xFFFF0000), F32).astype(BF16)
    return lo, hi


def _split3(x):
    hi = x.astype(BF16)
    r1 = x - hi.astype(F32)
    mid = r1.astype(BF16)
    lo = (r1 - mid.astype(F32)).astype(BF16)
    return hi, mid, lo


def _rmsnorm_kernel(x_ref, w_ref, wz_ref, o_ref, z_ref, *, eps):
    x = x_ref[...]
    ms = jnp.mean(x * x, axis=-1, keepdims=True)
    h = (x * lax.rsqrt(ms + eps) * w_ref[...]).astype(o_ref.dtype)
    o_ref[...] = h
    z_ref[...] = _dot_nt(h, wz_ref[...].astype(BF16))


def _rmsnorm(x, w, wt, z_row0, eps, out_dtype, tm=512):
    n, d = x.shape
    assert z_row0 % LANES == 0
    return pl.pallas_call(
        functools.partial(_rmsnorm_kernel, eps=eps),
        grid=(n // tm,),
        in_specs=[pl.BlockSpec((tm, d), lambda i: (i, 0)),
                  pl.BlockSpec((1, d), lambda i: (0, 0)),
                  pl.BlockSpec((LANES, d), lambda i: (z_row0 // LANES, 0))],
        out_specs=[pl.BlockSpec((tm, d), lambda i: (i, 0)),
                   pl.BlockSpec((tm, LANES), lambda i: (i, 0))],
        out_shape=[jax.ShapeDtypeStruct((n, d), out_dtype), jax.ShapeDtypeStruct((n, LANES), F32)],
        compiler_params=_params(("parallel",), 32),
        name="rmsnorm",
    )(x, w.reshape(1, d), wt)


def _proj_kernel(*refs, shift, rope, n_q_tiles, q_scale):
    h_ref, wa_ref = refs[0], refs[1]
    pos = 2
    wb_ref = None
    if shift:
        wb_ref = refs[pos]
        pos += 1
    if rope:
        cos_ref, sin_ref = refs[pos], refs[pos + 1]
        pos += 2
    o_ref, w_scr = refs[pos], refs[pos + 1]
    tn = wa_ref.shape[0]

    @pl.when(pl.program_id(1) == 0)
    def _():
        if shift:
            w_scr[:tn - shift, :] = wa_ref[shift:, :].astype(BF16)
            w_scr[tn - shift:, :] = wb_ref[...].astype(BF16)
        else:
            w_scr[...] = wa_ref[...].astype(BF16)

    y = _dot_nt(h_ref[...], w_scr[...])
    if not rope:
        o_ref[...] = y.astype(o_ref.dtype)
        return
    tm = y.shape[0]
    scale = jnp.where(pl.program_id(0) < n_q_tiles, q_scale, 1.0).astype(F32)
    cos = cos_ref[...]
    sin = sin_ref[...]
    lane = lax.broadcasted_iota(I32, (tm, LANES), 1)
    first_half = (lane & (DIFF_HD - 1)) < (DIFF_HD // 2)
    for c in range(tn // LANES):
        yc = y[:, c * LANES:(c + 1) * LANES]
        sw = jnp.where(first_half,
                       pltpu.roll(yc, LANES - DIFF_HD // 2, 1),
                       pltpu.roll(yc, DIFF_HD // 2, 1))
        o_ref[:, c * LANES:(c + 1) * LANES] = ((yc * cos + sw * sin) * scale).astype(o_ref.dtype)


BF16_SUBLANES = 16


def _proj(h, wt, col0, width, out_dtype, tm=1024, tn=1024, rope=None):
    n, k = h.shape
    tn = min(tn, width)
    shift = col0 % LANES
    base = col0 - shift
    assert base % tn == 0 and width % tn == 0 and shift % BF16_SUBLANES == 0
    in_specs = [pl.BlockSpec((tm, k), lambda j, i: (i, 0)),
                pl.BlockSpec((tn, k), lambda j, i: (base // tn + j, 0))]
    args = [h, wt]
    if shift:
        assert (base + tn) % shift == 0
        in_specs.append(pl.BlockSpec((shift, k), lambda j, i: ((base + (j + 1) * tn) // shift, 0)))
        args.append(wt)
    n_q_tiles, q_scale = 0, 1.0
    if rope is not None:
        cos, sin, n_q_cols, q_scale, seq = rope
        n_q_tiles = n_q_cols // tn
        blocks_per_seq = seq // tm
        in_specs += [pl.BlockSpec((tm, LANES), lambda j, i: (i % blocks_per_seq, 0)),
                     pl.BlockSpec((tm, LANES), lambda j, i: (i % blocks_per_seq, 0))]
        args += [cos, sin]
    return pl.pallas_call(
        functools.partial(_proj_kernel, shift=shift, rope=rope is not None,
                          n_q_tiles=n_q_tiles, q_scale=q_scale),
        grid=(width // tn, n // tm),
        in_specs=in_specs,
        out_specs=pl.BlockSpec((tm, tn), lambda j, i: (i, j)),
        out_shape=jax.ShapeDtypeStruct((n, width), out_dtype),
        scratch_shapes=[pltpu.VMEM((tn, k), BF16)],
        compiler_params=_params(("parallel", "arbitrary"), 52),
        name="proj_rope" if rope is not None else "proj",
    )(*args)


GLA_BLOCK = 256
GLA_HEADS_PER_STEP = 4
GLA_STAGES = 3


def _log_sigmoid(x):
    return jnp.minimum(x, 0.0) - jnp.log1p(jnp.exp(-jnp.abs(x)))


def _gla_block(q, k, v, g, st_ref, forward):
    L = GLA_CHUNK
    n = GLA_BLOCK
    n_chunks = n // L
    row = lax.broadcasted_iota(I32, (n, n), 0)
    col = lax.broadcasted_iota(I32, (n, n), 1)
    rowc = row >> (L.bit_length() - 1)
    colc = col >> (L.bit_length() - 1)
    same = rowc == colc
    if forward:
        tri = jnp.where(same & (col <= row), 1.0, 0.0).astype(BF16)
        diag_mask = same & (col <= row)
        order = list(range(n_chunks))
        mid, last = L // 2 - 1, L - 1
    else:
        tri = jnp.where(same & (col >= row), 1.0, 0.0).astype(BF16)
        diag_mask = same & (col > row)
        order = list(range(n_chunks - 1, -1, -1))
        mid, last = L // 2, 0

    g_hi, g_mid, g_lo = _split3(g)
    b = _dot(tri, g_hi) + _dot(tri, g_mid) + _dot(tri, g_lo)
    yield None
    dk = b.shape[1]
    chunks = range(n_chunks)
    b_mid = [b[c * L + mid:c * L + mid + 1, :] for c in chunks]
    b_last = [b[c * L + last:c * L + last + 1, :] for c in chunks]
    scanned = {}
    run = jnp.zeros((1, dk), F32)
    for c in order:
        scanned[c] = run
        run = run + b_last[c]
    b_tot = run
    rows_of = lambda vals: jnp.concatenate([jnp.broadcast_to(x, (L, dk)) for x in vals], axis=0)

    mid_full = rows_of(b_mid)
    qe = q * jnp.exp2(b - mid_full)
    ki = k * jnp.exp2(mid_full - b)
    ke_loc = (ki * rows_of([jnp.exp2(b_last[c] - b_mid[c]) for c in chunks])).astype(BF16)
    chunk_rows = lambda a, c: a[c * L:(c + 1) * L, :]
    qx_parts, ke_parts = [], []
    for c_src in order[:-1]:
        ref = scanned[c_src] + b_last[c_src]
        later = [(c > c_src) if forward else (c < c_src) for c in chunks]
        qx_parts.append(jnp.concatenate(
            [(chunk_rows(qe, c) * jnp.exp2(b_mid[c] + scanned[c] - ref)).astype(BF16) if later[c]
             else jnp.zeros((L, dk), BF16) for c in chunks], axis=0))
        ke_parts.append(jnp.concatenate(
            [chunk_rows(ke_loc, c) if c == c_src else jnp.zeros((L, dk), BF16) for c in chunks], axis=0))
    cross = _dot_nt(jnp.concatenate(qx_parts, axis=1), jnp.concatenate(ke_parts, axis=1))
    p = jnp.where(diag_mask, _dot_nt(qe.astype(BF16), ki.astype(BF16)), cross)
    yield None
    vb = v.astype(BF16)
    st = st_ref[...]
    qb = qe * rows_of([jnp.exp2(b_mid[c] + scanned[c]) for c in chunks])
    o = _dot(p.astype(BF16), vb) + _dot_nt(qb.astype(BF16), st.astype(BF16))
    ke = ki * rows_of([jnp.exp2(b_tot - scanned[c] - b_mid[c]) for c in chunks])
    st_ref[...] = st * jnp.exp2(b_tot) + _dot_tn(vb, ke.astype(BF16))
    yield o


def _gla_kernel(qf_ref, kf_ref, vf_ref, zf_ref, qb_ref, kb_ref, vb_ref, zb_ref,
                upf_ref, biasf_ref, upb_ref, biasb_ref, of_ref, ob_ref, sf_ref, sb_ref,
                *, q_scale):
    @pl.when(pl.program_id(2) == 0)
    def _():
        sf_ref[...] = jnp.zeros_like(sf_ref)
        sb_ref[...] = jnp.zeros_like(sb_ref)

    def gate(z_ref, up_ref, bias_ref):
        z = z_ref[...]
        z = jnp.where(lax.broadcasted_iota(I32, z.shape, 1) < 2 * GLA_GATE_RANK, z, 0.0)
        pre = _dot(z.astype(BF16), up_ref[...]) + bias_ref[...]
        return _log_sigmoid(pre) * (math.log2(math.e) / GLA_GATE_NORMALIZER)

    g_f = gate(zf_ref, upf_ref, biasf_ref)
    g_b = gate(zb_ref, upb_ref, biasb_ref)

    dv, dk = sf_ref.shape[1:]
    blocks = []
    for hh in range(GLA_HEADS_PER_STEP):
        ck = slice(hh * dk, (hh + 1) * dk)
        cv = slice(hh * dv, (hh + 1) * dv)
        blocks.append((of_ref, cv, _gla_block(qf_ref[:, ck] * q_scale, kf_ref[:, ck], vf_ref[:, cv],
                                              g_f[:, ck], sf_ref.at[hh], True)))
        blocks.append((ob_ref, cv, _gla_block(qb_ref[:, ck] * q_scale, kb_ref[:, ck], vb_ref[:, cv],
                                              g_b[:, ck], sb_ref.at[hh], False)))
    for _ in range(GLA_STAGES - 1):
        for _, _, blk in blocks:
            next(blk)
    for out_ref, cv, blk in blocks:
        out_ref[:, cv] = next(blk)


def _gla(pm, z, upf, biasf, upb, biasb, batch, seq, dk, dv):
    n = pm.shape[0]
    nb = seq // GLA_BLOCK
    H = GLA_HEADS
    hs = GLA_HEADS_PER_STEP
    kw = H * dk
    fwd = lambda b, h, i: b * nb + i
    bwd = lambda b, h, i: b * nb + nb - 1 - i
    k_col = kw // (hs * dk)
    v_col = 2 * kw // (hs * dv)

    def specs(rowf):
        return [pl.BlockSpec((GLA_BLOCK, hs * dk), lambda b, h, i: (rowf(b, h, i), h)),
                pl.BlockSpec((GLA_BLOCK, hs * dk), lambda b, h, i: (rowf(b, h, i), k_col + h)),
                pl.BlockSpec((GLA_BLOCK, hs * dv), lambda b, h, i: (rowf(b, h, i), v_col + h)),
                pl.BlockSpec((GLA_BLOCK, LANES), lambda b, h, i: (rowf(b, h, i), 0))]

    w_specs = [pl.BlockSpec((LANES, hs * dk), lambda b, h, i: (0, h)),
               pl.BlockSpec((1, hs * dk), lambda b, h, i: (0, h))]
    out_shape = jax.ShapeDtypeStruct((n, H * dv), F32)
    return pl.pallas_call(
        functools.partial(_gla_kernel, q_scale=dk ** -0.5),
        grid=(batch, H // hs, nb),
        in_specs=specs(fwd) + specs(bwd) + w_specs + w_specs,
        out_specs=[pl.BlockSpec((GLA_BLOCK, hs * dv), lambda b, h, i: (fwd(b, h, i), h)),
                   pl.BlockSpec((GLA_BLOCK, hs * dv), lambda b, h, i: (bwd(b, h, i), h))],
        out_shape=[out_shape, out_shape],
        scratch_shapes=[pltpu.VMEM((hs, dv, dk), F32), pltpu.VMEM((hs, dv, dk), F32)],
        compiler_params=_params(("parallel", "parallel", "arbitrary"), 48),
        name="gla",
    )(pm, pm, pm, z, pm, pm, pm, z, upf, biasf, upb, biasb)


ATTN_TQ = 256
ATTN_TK = 256
ATTN_SUB = 64
ATTN_VROWS = 2 * DIFF_HD + 16
ATTN_VT_CHUNK = 512
ATTN_HEADS_PER_STEP = 2


def _diff_attn_kernel(q_ref, k_ref, v_ref, lam_ref, w_ref, o_ref, s_ref, vt_ref, *, lam_init, eps):
    seq = k_ref.shape[0]
    vd = 2 * DIFF_HD
    heads = range(ATTN_HEADS_PER_STEP)
    cols = [slice(h * vd, (h + 1) * vd) for h in heads]
    pad_row = lax.broadcasted_iota(I32, (ATTN_VROWS - vd, seq), 0)
    for h in heads:
        for r in range(0, seq, ATTN_VT_CHUNK):
            vt_ref[h, :vd, r:r + ATTN_VT_CHUNK] = v_ref[r:r + ATTN_VT_CHUNK, cols[h]].astype(F32).T.astype(BF16)
        vt_ref[h, vd:, :] = jnp.where(pad_row == 0, 1.0, 0.0).astype(BF16)

    lv = lam_ref[...]
    lam = (jnp.exp(jnp.sum(lv[0:1] * lv[1:2], axis=-1, keepdims=True))
           - jnp.exp(jnp.sum(lv[2:3] * lv[3:4], axis=-1, keepdims=True)) + lam_init)
    lane = lax.broadcasted_iota(I32, (ATTN_TQ, vd), 1)
    w_col = w_ref[...]

    n_tiles = seq // ATTN_TQ
    n_chunks = seq // ATTN_TK
    groups = ATTN_TK // 8

    def masked_q(h, tile, c):
        q = q_ref[pl.ds(pl.multiple_of(tile * ATTN_TQ, ATTN_TQ), ATTN_TQ), cols[h]]
        keep = (lane < DIFF_HD) if c == 0 else (lane >= DIFF_HD)
        return jnp.where(keep, q, jnp.zeros_like(q))

    def score_chunk(h, qc, c, j, mrun):
        st = _dot_nt(k_ref[j * ATTN_TK:(j + 1) * ATTN_TK, cols[h]], qc)
        s_ref[h, c, j * ATTN_TK:(j + 1) * ATTN_TK, :] = st
        return jnp.maximum(mrun, jnp.max(st.reshape(groups, 8, ATTN_TQ), axis=0))

    def value_chunk(h, c, j, m, acc):
        pieces = [jnp.exp2(s_ref[h, c, r:r + ATTN_SUB, :] - m).astype(BF16)
                  for r in range(j * ATTN_TK, (j + 1) * ATTN_TK, ATTN_SUB)]
        return acc + _dot(vt_ref[h, :, j * ATTN_TK:(j + 1) * ATTN_TK], jnp.concatenate(pieces, axis=0))

    def scores(h, tile, c):
        qc = masked_q(h, tile, c)
        mrun = jnp.full((8, ATTN_TQ), -jnp.inf, F32)
        for j in range(n_chunks):
            mrun = score_chunk(h, qc, c, j, mrun)
        return jnp.max(mrun, axis=0, keepdims=True)

    def overlapped(h, c_val, m, tile, c_score):
        qc = masked_q(h, tile, c_score)
        mrun = jnp.full((8, ATTN_TQ), -jnp.inf, F32)
        acc = jnp.zeros((ATTN_VROWS, ATTN_TQ), F32)
        for j in range(n_chunks):
            acc = value_chunk(h, c_val, j, m, acc)
        for j in range(n_chunks):
            mrun = score_chunk(h, qc, c_score, j, mrun)
        o = acc[:vd] / acc[vd:vd + 1]
        return o, jnp.max(mrun, axis=0, keepdims=True)

    def q_tile(i, m0):
        nxt = jnp.minimum(i + 1, n_tiles - 1)
        first = [overlapped(h, 0, m0[h], i, 1) for h in heads]
        second = [overlapped(h, 1, first[h][1], nxt, 0) for h in heads]
        for h in heads:
            o = first[h][0] - lam * second[h][0]
            ms = jnp.mean(o * o, axis=0, keepdims=True)
            y = (o * lax.rsqrt(ms + eps) * w_col) * (1.0 - lam_init)
            o_ref[pl.ds(pl.multiple_of(i * ATTN_TQ, ATTN_TQ), ATTN_TQ), cols[h]] = y.T
        return tuple(second[h][1] for h in heads)

    lax.fori_loop(0, n_tiles, q_tile, tuple(scores(h, 0, 0) for h in heads))


def _diff_attn(qk, v, lam_vecs, subln_w, batch, seq, heads, lam_init):
    n = qk.shape[0]
    vd = 2 * DIFF_HD
    hs = ATTN_HEADS_PER_STEP
    steps = heads // hs
    return pl.pallas_call(
        functools.partial(_diff_attn_kernel, lam_init=lam_init, eps=SUBLN_EPS),
        grid=(batch, steps),
        in_specs=[pl.BlockSpec((seq, hs * vd), lambda b, h: (b, h)),
                  pl.BlockSpec((seq, hs * vd), lambda b, h: (b, steps + h)),
                  pl.BlockSpec((seq, hs * vd), lambda b, h: (b, h)),
                  pl.BlockSpec((8, LANES), lambda b, h: (0, 0)),
                  pl.BlockSpec((vd, 1), lambda b, h: (0, 0))],
        out_specs=pl.BlockSpec((seq, hs * vd), lambda b, h: (b, h)),
        out_shape=jax.ShapeDtypeStruct((n, heads * vd), F32),
        scratch_shapes=[pltpu.VMEM((hs, 2, seq, ATTN_TQ), F32), pltpu.VMEM((hs, ATTN_VROWS, seq), BF16)],
        compiler_params=_params(("parallel", "parallel"), 48),
        name="diff_attn",
    )(qk, qk, v, lam_vecs, subln_w.reshape(vd, 1))


def _merge_kernel(of_ref, ob_ref, r_ref, yb_ref, ga_ref, gb_ref, x_ref, wout_ref, gnw_ref,
                  fnw_ref, wr_ref, x1_ref, h2_ref, aff_ref, *, dv, n_experts):
    o = of_ref[...] + ob_ref[...]
    segs = []
    for hh in range(o.shape[1] // dv):
        seg = o[:, hh * dv:(hh + 1) * dv]
        ms = jnp.mean(seg * seg, axis=-1, keepdims=True)
        segs.append(seg * lax.rsqrt(ms + NORM_EPS))
    on = jnp.concatenate(segs, axis=1) * gnw_ref[...]
    r = r_ref[...]
    ya = on * (r * jax.nn.sigmoid(r))
    merged = jax.nn.sigmoid(ga_ref[...]) * ya + jax.nn.sigmoid(gb_ref[...]) * yb_ref[...]
    x1 = x_ref[...] + _dot(merged.astype(BF16), wout_ref[...])
    x1_ref[...] = x1
    ms = jnp.mean(x1 * x1, axis=-1, keepdims=True)
    h2 = (x1 * lax.rsqrt(ms + NORM_EPS) * fnw_ref[...]).astype(BF16)
    h2_ref[...] = _pack_halves(h2)
    logits = _dot(h2, wr_ref[...])
    lane = lax.broadcasted_iota(I32, logits.shape, 1)
    logits = jnp.where(lane < n_experts, logits, -jnp.inf)
    e = jnp.exp(logits - jnp.max(logits, axis=-1, keepdims=True))
    aff_ref[...] = e / jnp.sum(e, axis=-1, keepdims=True)


def _merge(o_f, o_b, pm, yb, gates, x, wout, gnw, fnw, wr, dv, tm=256):
    n, d = x.shape
    r_col = (pm.shape[1] - d) // d
    row = lambda i: (i, 0)
    const = lambda i: (0, 0)
    return pl.pallas_call(
        functools.partial(_merge_kernel, dv=dv, n_experts=N_EXPERTS),
        grid=(n // tm,),
        in_specs=[pl.BlockSpec((tm, d), row), pl.BlockSpec((tm, d), row),
                  pl.BlockSpec((tm, d), lambda i: (i, r_col)),
                  pl.BlockSpec((tm, d), row),
                  pl.BlockSpec((tm, d), lambda i: (i, 0)), pl.BlockSpec((tm, d), lambda i: (i, 1)),
                  pl.BlockSpec((tm, d), row),
                  pl.BlockSpec((d, d), const, pipeline_mode=pl.Buffered(1)),
                  pl.BlockSpec((1, d), const),
                  pl.BlockSpec((1, d), const), pl.BlockSpec((d, LANES), const)],
        out_specs=[pl.BlockSpec((tm, d), row), pl.BlockSpec((tm, d // 2), row),
                   pl.BlockSpec((tm, LANES), row)],
        out_shape=[jax.ShapeDtypeStruct((n, d), F32), jax.ShapeDtypeStruct((n, d // 2), U32),
                   jax.ShapeDtypeStruct((n, LANES), F32)],
        compiler_params=_params(("parallel",), 56),
        name="merge_outproj",
    )(o_f, o_b, pm, yb, gates, gates, x, wout, gnw, fnw, wr)


TOPK_LANE_BLOCK = 256


def _exclusive_prefix_count(flags, strict_upper):
    e, t = flags.shape
    carry = jnp.zeros((e, 1), F32)
    parts = []
    for blk in range(t // TOPK_LANE_BLOCK):
        f = flags[:, blk * TOPK_LANE_BLOCK:(blk + 1) * TOPK_LANE_BLOCK]
        parts.append(_dot(f.astype(BF16), strict_upper) + carry)
        carry = carry + jnp.sum(f, axis=-1, keepdims=True)
    return jnp.concatenate(parts, axis=1)


def _topk_kernel(aff_ref, idx_ref, g_ref, posm_ref, *, cap, n_exp):
    av = aff_ref[0]
    t = av.shape[0]
    a = av.T[:n_exp, :]
    bits = pltpu.bitcast(a, I32)

    def search(i, thr):
        cand = thr | jnp.left_shift(jnp.int32(1), 30 - i)
        cnt = jnp.sum(jnp.where(bits >= cand, 1.0, 0.0), axis=-1, keepdims=True)
        return jnp.where(cnt >= cap, cand, thr)

    thr = lax.fori_loop(0, 31, search, jnp.zeros((n_exp, 1), I32))
    gt = jnp.where(bits > thr, 1.0, 0.0)
    eq = jnp.where(bits == thr, 1.0, 0.0)
    need = cap - jnp.sum(gt, axis=-1, keepdims=True)

    r = lax.broadcasted_iota(I32, (TOPK_LANE_BLOCK, TOPK_LANE_BLOCK), 0)
    c = lax.broadcasted_iota(I32, (TOPK_LANE_BLOCK, TOPK_LANE_BLOCK), 1)
    strict_upper = jnp.where(r < c, 1.0, 0.0).astype(BF16)

    tie_rank = _exclusive_prefix_count(eq, strict_upper)
    sel = gt + eq * jnp.where(tie_rank < need, 1.0, 0.0)
    pos = _exclusive_prefix_count(sel, strict_upper)
    posm_ref[...] = jnp.where(sel > 0.5, pos, -1.0)

    hi, mid, lo = _split3(av)
    lane = lax.broadcasted_iota(I32, av.shape, 1)
    tok = lax.broadcasted_iota(I32, av.shape, 0)
    digits = jnp.where(lane == 0, (tok >> 6).astype(F32),
                       jnp.where(lane == 1, (tok & 63).astype(F32), 0.0))
    feat = (digits + pltpu.roll(hi.astype(F32), 2, 1) + pltpu.roll(mid.astype(F32), 2 + n_exp, 1)
            + pltpu.roll(lo.astype(F32), 2 + 2 * n_exp, 1)).astype(BF16)

    slot = lax.broadcasted_iota(I32, (cap, t), 0).astype(F32)
    out_lane = lax.broadcasted_iota(I32, (cap, LANES), 1)

    def per_expert(e, _):
        onehot = jnp.where(posm_ref[pl.ds(e, 1), :] == slot, 1.0, 0.0).astype(BF16)
        res = _dot(onehot, feat)
        idx_ref[0, e] = (res[:, 0:1] * 64.0 + res[:, 1:2]).astype(I32)
        mine = (out_lane == 2 + e) | (out_lane == 2 + n_exp + e) | (out_lane == 2 + 2 * n_exp + e)
        g_ref[0, e] = jnp.sum(jnp.where(mine, res, 0.0), axis=-1, keepdims=True)
        return 0

    lax.fori_loop(0, n_exp, per_expert, 0, unroll=2)


def _topk(aff, cap, n_exp):
    batch, t, _ = aff.shape
    return pl.pallas_call(
        functools.partial(_topk_kernel, cap=cap, n_exp=n_exp),
        grid=(batch,),
        in_specs=[pl.BlockSpec((1, t, LANES), lambda b: (b, 0, 0))],
        out_specs=[pl.BlockSpec((1, n_exp, cap, 1), lambda b: (b, 0, 0, 0)),
                   pl.BlockSpec((1, n_exp, cap, 1), lambda b: (b, 0, 0, 0))],
        out_shape=[jax.ShapeDtypeStruct((batch, n_exp, cap, 1), I32),
                   jax.ShapeDtypeStruct((batch, n_exp, cap, 1), F32)],
        scratch_shapes=[pltpu.VMEM((n_exp, t), F32)],
        compiler_params=_params(("parallel",), 48),
        name="expert_topk",
    )(aff)


def _dispatch_kernel(idx_ref, h_ref, o_ref, *, cap, n_exp):
    base = (pl.program_id(0) * n_exp + pl.program_id(1)) * cap

    def body(c, _):
        o_ref[pl.ds(c, 1), :] = h_ref[pl.ds(idx_ref[base + c], 1), :]
        return 0

    lax.fori_loop(0, cap, body, 0, unroll=8)


def _dispatch(idx_flat, hp, batch, seq, cap):
    width = hp.shape[1]
    return pl.pallas_call(
        functools.partial(_dispatch_kernel, cap=cap, n_exp=N_EXPERTS),
        grid_spec=pltpu.PrefetchScalarGridSpec(
            num_scalar_prefetch=1,
            grid=(batch, N_EXPERTS),
            in_specs=[pl.BlockSpec((seq, width), lambda b, e, idx: (b, 0))],
            out_specs=pl.BlockSpec((None, cap, width), lambda b, e, idx: (e, b, 0))),
        out_shape=jax.ShapeDtypeStruct((N_EXPERTS, batch * cap, width), hp.dtype),
        compiler_params=_params(("arbitrary", "arbitrary"), 48),
        name="dispatch",
    )(idx_flat, hp)


def _ffn_kernel(x_ref, wg_ref, wu_ref, wd_ref, g_ref, o_ref, xs_ref):
    f = pl.program_id(1)
    d = xs_ref.shape[1]

    @pl.when(f == 0)
    def _():
        lo, hi = _unpack_halves(x_ref[...])
        xs_ref[:, :d // 2] = lo
        xs_ref[:, d // 2:] = hi
        o_ref[...] = jnp.zeros_like(o_ref)

    x = xs_ref[...]
    a = _dot(x, wg_ref[...].astype(BF16))
    u = _dot(x, wu_ref[...].astype(BF16))
    hid = (a * jax.nn.sigmoid(a) * u).astype(BF16)
    o_ref[...] += _dot(hid, wd_ref[...].astype(BF16))

    @pl.when(f == pl.num_programs(1) - 1)
    def _():
        o_ref[...] = o_ref[...] * g_ref[...].reshape(o_ref.shape[0], 1)


def _ffn(xe, wg, wu, wd, g, tf=256):
    n_exp, rows, half = xe.shape
    batch, _, cap, _ = g.shape
    d = 2 * half
    dff = wg.shape[2]
    return pl.pallas_call(
        _ffn_kernel,
        grid=(n_exp, dff // tf),
        in_specs=[pl.BlockSpec((None, rows, half), lambda e, f: (e, 0, 0)),
                  pl.BlockSpec((None, d, tf), lambda e, f: (e, 0, f)),
                  pl.BlockSpec((None, d, tf), lambda e, f: (e, 0, f)),
                  pl.BlockSpec((None, tf, d), lambda e, f: (e, f, 0)),
                  pl.BlockSpec((batch, None, cap, 1), lambda e, f: (0, e, 0, 0))],
        out_specs=pl.BlockSpec((None, rows, d), lambda e, f: (e, 0, 0)),
        out_shape=jax.ShapeDtypeStruct((n_exp, rows, d), F32),
        scratch_shapes=[pltpu.VMEM((rows, d), BF16)],
        compiler_params=_params(("parallel", "arbitrary"), 56),
        name="expert_ffn",
    )(xe, wg, wu, wd, g)


COMBINE_UNROLL = 8


def _combine_kernel(idx_ref, x1_ref, ye_ref, w_ref, o_ref, acc_ref, *, cap, n_exp):
    b = pl.program_id(0)
    s = pl.program_id(1)
    chunk = o_ref.shape[0]

    @pl.when(s == 0)
    def _():
        acc_ref[...] = jnp.zeros_like(acc_ref)

    @pl.when(s < n_exp)
    def _():
        rows = pl.ds(pl.multiple_of(s * chunk, chunk), chunk)
        acc_ref[rows, :] = acc_ref[rows, :] + x1_ref[...]
        base = (b * n_exp + s) * cap

        def group(i, _):
            c = i * COMBINE_UNROLL
            dst = [pl.ds(idx_ref[base + c + k], 1) for k in range(COMBINE_UNROLL)]
            sums = [acc_ref[dst[k], :] + ye_ref[pl.ds(c + k, 1), :] for k in range(COMBINE_UNROLL)]
            for k in range(COMBINE_UNROLL):
                acc_ref[dst[k], :] = sums[k]
            return 0

        lax.fori_loop(0, cap // COMBINE_UNROLL, group, 0)

    @pl.when(s >= n_exp)
    def _():
        x = acc_ref[pl.ds(pl.multiple_of((s - n_exp) * chunk, chunk), chunk), :]
        ms = jnp.mean(x * x, axis=-1, keepdims=True)
        o_ref[...] = x * lax.rsqrt(ms + NORM_EPS) * w_ref[...]


def _combine(idx_flat, x1, ye, w, batch, seq, cap):
    n, d = x1.shape
    n_exp = N_EXPERTS
    chunk = seq // n_exp
    assert cap % COMBINE_UNROLL == 0
    return pl.pallas_call(
        functools.partial(_combine_kernel, cap=cap, n_exp=n_exp),
        grid_spec=pltpu.PrefetchScalarGridSpec(
            num_scalar_prefetch=1,
            grid=(batch, 2 * n_exp),
            in_specs=[pl.BlockSpec((chunk, d), lambda b, s, idx: (b * n_exp + jnp.minimum(s, n_exp - 1), 0)),
                      pl.BlockSpec((None, cap, d), lambda b, s, idx: (jnp.minimum(s, n_exp - 1), b, 0)),
                      pl.BlockSpec((1, d), lambda b, s, idx: (0, 0))],
            out_specs=pl.BlockSpec((chunk, d), lambda b, s, idx: (b * n_exp + jnp.maximum(s - n_exp, 0), 0)),
            scratch_shapes=[pltpu.VMEM((seq, d), F32)]),
        out_shape=jax.ShapeDtypeStruct((n, d), F32),
        compiler_params=_params(("arbitrary", "arbitrary"), 56),
        name="combine_norm",
    )(idx_flat, x1, ye, w.reshape(1, d))


def _rope_tables(seq):
    pos = jnp.arange(seq, dtype=F32)
    inv_freq = ROPE_THETA ** (-jnp.arange(0, DIFF_HD, 2, dtype=F32) / DIFF_HD)
    ang = pos[:, None] * inv_freq[None, :]
    cos, sin = jnp.cos(ang), jnp.sin(ang)
    reps = LANES // (DIFF_HD // 2)
    sign = jnp.tile(jnp.concatenate([-jnp.ones((DIFF_HD // 2,), F32), jnp.ones((DIFF_HD // 2,), F32)]),
                    LANES // DIFF_HD)
    return jnp.tile(cos, (1, reps)), jnp.tile(sin, (1, reps)) * sign[None, :]


def kernel(x, norm_mix_w, w_in, gla_gate_up_fwd, gla_gate_bias_fwd, gla_gate_up_bwd, gla_gate_bias_bwd, gla_norm_w, diff_lambda_q1, diff_lambda_k1, diff_lambda_q2, diff_lambda_k2, diff_subln_w, w_out, norm_ffn_w, w_router, w_gate_e, w_up_e, w_down_e, norm_final_w):
    batch, seq, d = x.shape
    depth = w_in.shape[0]
    assert depth == 1, "the combine stage applies the final norm, so it must follow the only layer"
    n = batch * seq
    kw = gla_gate_up_fwd.shape[2]
    dk = kw // GLA_HEADS
    dv = d // GLA_HEADS
    rank = gla_gate_up_fwd.shape[1]
    diff_heads = d // (2 * DIFF_HD)
    cap = CAPACITY_FACTOR * seq // N_EXPERTS
    cos_t, sin_t = _rope_tables(seq)

    o_z = 2 * kw + 2 * d
    o_dq = o_z + 2 * rank
    o_dv = o_dq + 2 * d
    o_gate = o_dv + d

    xf = x.reshape(n, d)
    for l in range(depth):
        w = jnp.transpose(w_in[l])
        h, z = _rmsnorm(xf, norm_mix_w[l], w, o_z, NORM_EPS, BF16)
        pm = _proj(h, w, 0, o_z, F32)
        qk = _proj(h, w, o_dq, 2 * d, BF16,
                   rope=(cos_t, sin_t, d, DIFF_HD ** -0.5 * math.log2(math.e), seq))
        vd = _proj(h, w, o_dv, d, BF16)
        gates = _proj(h, w, o_gate, 2 * d, F32)

        zero_rows = lambda a, before: jnp.pad(a, ((before, LANES - rank - before), (0, 0))).astype(BF16)
        o_f, o_b = _gla(pm, z, zero_rows(gla_gate_up_fwd[l], 0), gla_gate_bias_fwd[l].reshape(1, kw),
                        zero_rows(gla_gate_up_bwd[l], rank), gla_gate_bias_bwd[l].reshape(1, kw),
                        batch, seq, dk, dv)

        lam_init = 0.8 - 0.6 * math.exp(-0.3 * l)
        lam_vecs = jnp.pad(jnp.stack([diff_lambda_q1[l], diff_lambda_k1[l],
                                      diff_lambda_q2[l], diff_lambda_k2[l]]).astype(F32),
                           ((0, 4), (0, LANES - DIFF_HD)))
        yb = _diff_attn(qk, vd, lam_vecs, diff_subln_w[l], batch, seq, diff_heads, lam_init)

        wr = jnp.pad(w_router[l], ((0, 0), (0, LANES - N_EXPERTS))).astype(BF16)
        x1, h2, aff = _merge(o_f, o_b, pm, yb, gates, xf, w_out[l].astype(BF16),
                             jnp.tile(gla_norm_w[l], GLA_HEADS).reshape(1, d),
                             norm_ffn_w[l].reshape(1, d), wr, dv)

        idx, g = _topk(aff.reshape(batch, seq, LANES), cap, N_EXPERTS)
        idx_flat = idx.reshape(batch * N_EXPERTS * cap)

        xe = _dispatch(idx_flat, h2, batch, seq, cap)
        ye = _ffn(xe, w_gate_e[l], w_up_e[l], w_down_e[l], g)

        xf = _combine(idx_flat, x1, ye, norm_final_w, batch, seq, cap)
    return xf.reshape(batch, seq, d)
```

```python
import functools
import math

import jax
import jax.numpy as jnp
from jax import lax
from jax.experimental import pallas as pl
from jax.experimental.pallas import tpu as pltpu

F32 = jnp.float32
BF16 = jnp.bfloat16
I32 = jnp.int32
U32 = jnp.uint32

GLA_HEADS = 4
GLA_GATE_RANK = 16
GLA_GATE_NORMALIZER = 16.0
GLA_CHUNK = 64
DIFF_HD = 64
ROPE_THETA = 10000.0
N_EXPERTS = 16
CAPACITY_FACTOR = 2
NORM_EPS = 1e-6
SUBLN_EPS = 1e-5

LANES = 128
VMEM_PHYSICAL = 64 * 1024 * 1024


def _params(semantics, vmem_mb):
    return pltpu.CompilerParams(dimension_semantics=semantics,
                                vmem_limit_bytes=vmem_mb * 1024 * 1024)


def _dot(a, b):
    return jnp.dot(a, b, preferred_element_type=F32)


def _dot_nt(a, b):
    return lax.dot_general(a, b, (((1,), (1,)), ((), ())), preferred_element_type=F32)


def _dot_tn(a, b):
    return lax.dot_general(a, b, (((0,), (0,)), ((), ())), preferred_element_type=F32)


def _pack_halves(x):
    n = x.shape[1] // 2
    lo = pltpu.bitcast(x[:, :n].astype(F32), U32)
    hi = pltpu.bitcast(x[:, n:].astype(F32), U32)
    return (lo >> 16) | (hi & jnp.uint32(0xFFFF0000))


def _unpack_halves(p):
    lo = pltpu.bitcast(p << 16, F32).astype(BF16)
    hi = pltpu.bitcast(p & jnp.uint32(0xFFFF0000), F32).astype(BF16)
    return lo, hi


def _split3(x):
    hi = x.astype(BF16)
    r1 = x - hi.astype(F32)
    mid = r1.astype(BF16)
    lo = (r1 - mid.astype(F32)).astype(BF16)
    return hi, mid, lo


def _rmsnorm_kernel(x_ref, w_ref, wz_ref, o_ref, z_ref, *, eps):
    x = x_ref[...]
    ms = jnp.mean(x * x, axis=-1, keepdims=True)
    h = (x * lax.rsqrt(ms + eps) * w_ref[...]).astype(o_ref.dtype)
    o_ref[...] = h
    z_ref[...] = _dot_nt(h, wz_ref[...].astype(BF16))


def _rmsnorm(x, w, wt, z_row0, eps, out_dtype, tm=512):
    n, d = x.shape
    assert z_row0 % LANES == 0
    return pl.pallas_call(
        functools.partial(_rmsnorm_kernel, eps=eps),
        grid=(n // tm,),
        in_specs=[pl.BlockSpec((tm, d), lambda i: (i, 0)),
                  pl.BlockSpec((1, d), lambda i: (0, 0)),
                  pl.BlockSpec((LANES, d), lambda i: (z_row0 // LANES, 0))],
        out_specs=[pl.BlockSpec((tm, d), lambda i: (i, 0)),
                   pl.BlockSpec((tm, LANES), lambda i: (i, 0))],
        out_shape=[jax.ShapeDtypeStruct((n, d), out_dtype), jax.ShapeDtypeStruct((n, LANES), F32)],
        compiler_params=_params(("parallel",), 32),
        name="rmsnorm",
    )(x, w.reshape(1, d), wt)


def _proj_kernel(*refs, shift, rope, n_q_tiles, q_scale):
    h_ref, wa_ref = refs[0], refs[1]
    pos = 2
    wb_ref = None
    if shift:
        wb_ref = refs[pos]
        pos += 1
    if rope:
        cos_ref, sin_ref = refs[pos], refs[pos + 1]
        pos += 2
    o_ref, w_scr = refs[pos], refs[pos + 1]
    tn = wa_ref.shape[0]

    @pl.when(pl.program_id(1) == 0)
    def _():
        if shift:
            w_scr[:tn - shift, :] = wa_ref[shift:, :].astype(BF16)
            w_scr[tn - shift:, :] = wb_ref[...].astype(BF16)
        else:
            w_scr[...] = wa_ref[...].astype(BF16)

    y = _dot_nt(h_ref[...], w_scr[...])
    if not rope:
        o_ref[...] = y.astype(o_ref.dtype)
        return
    tm = y.shape[0]
    scale = jnp.where(pl.program_id(0) < n_q_tiles, q_scale, 1.0).astype(F32)
    cos = cos_ref[...]
    sin = sin_ref[...]
    lane = lax.broadcasted_iota(I32, (tm, LANES), 1)
    first_half = (lane & (DIFF_HD - 1)) < (DIFF_HD // 2)
    for c in range(tn // LANES):
        yc = y[:, c * LANES:(c + 1) * LANES]
        sw = jnp.where(first_half,
                       pltpu.roll(yc, LANES - DIFF_HD // 2, 1),
                       pltpu.roll(yc, DIFF_HD // 2, 1))
        o_ref[:, c * LANES:(c + 1) * LANES] = ((yc * cos + sw * sin) * scale).astype(o_ref.dtype)


BF16_SUBLANES = 16


def _proj(h, wt, col0, width, out_dtype, tm=1024, tn=1024, rope=None):
    n, k = h.shape
    tn = min(tn, width)
    shift = col0 % LANES
    base = col0 - shift
    assert base % tn == 0 and width % tn == 0 and shift % BF16_SUBLANES == 0
    in_specs = [pl.BlockSpec((tm, k), lambda j, i: (i, 0)),
                pl.BlockSpec((tn, k), lambda j, i: (base // tn + j, 0))]
    args = [h, wt]
    if shift:
        assert (base + tn) % shift == 0
        in_specs.append(pl.BlockSpec((shift, k), lambda j, i: ((base + (j + 1) * tn) // shift, 0)))
        args.append(wt)
    n_q_tiles, q_scale = 0, 1.0
    if rope is not None:
        cos, sin, n_q_cols, q_scale, seq = rope
        n_q_tiles = n_q_cols // tn
        blocks_per_seq = seq // tm
        in_specs += [pl.BlockSpec((tm, LANES), lambda j, i: (i % blocks_per_seq, 0)),
                     pl.BlockSpec((tm, LANES), lambda j, i: (i % blocks_per_seq, 0))]
        args += [cos, sin]
    return pl.pallas_call(
        functools.partial(_proj_kernel, shift=shift, rope=rope is not None,
                          n_q_tiles=n_q_tiles, q_scale=q_scale),
        grid=(width // tn, n // tm),
        in_specs=in_specs,
        out_specs=pl.BlockSpec((tm, tn), lambda j, i: (i, j)),
        out_shape=jax.ShapeDtypeStruct((n, width), out_dtype),
        scratch_shapes=[pltpu.VMEM((tn, k), BF16)],
        compiler_params=_params(("parallel", "arbitrary"), 52),
        name="proj_rope" if rope is not None else "proj",
    )(*args)


GLA_BLOCK = 256
GLA_HEADS_PER_STEP = 4
GLA_STAGES = 3


def _log_sigmoid(x):
    return jnp.minimum(x, 0.0) - jnp.log1p(jnp.exp(-jnp.abs(x)))


def _gla_block(q, k, v, g, st_ref, forward):
    L = GLA_CHUNK
    n = GLA_BLOCK
    n_chunks = n // L
    row = lax.broadcasted_iota(I32, (n, n), 0)
    col = lax.broadcasted_iota(I32, (n, n), 1)
    rowc = row >> (L.bit_length() - 1)
    colc = col >> (L.bit_length() - 1)
    same = rowc == colc
    if forward:
        tri = jnp.where(same & (col <= row), 1.0, 0.0).astype(BF16)
        diag_mask = same & (col <= row)
        order = list(range(n_chunks))
        mid, last = L // 2 - 1, L - 1
    else:
        tri = jnp.where(same & (col >= row), 1.0, 0.0).astype(BF16)
        diag_mask = same & (col > row)
        order = list(range(n_chunks - 1, -1, -1))
        mid, last = L // 2, 0

    g_hi, g_mid, g_lo = _split3(g)
    b = _dot(tri, g_hi) + _dot(tri, g_mid) + _dot(tri, g_lo)
    yield None
    dk = b.shape[1]
    chunks = range(n_chunks)
    b_mid = [b[c * L + mid:c * L + mid + 1, :] for c in chunks]
    b_last = [b[c * L + last:c * L + last + 1, :] for c in chunks]
    scanned = {}
    run = jnp.zeros((1, dk), F32)
    for c in order:
        scanned[c] = run
        run = run + b_last[c]
    b_tot = run
    rows_of = lambda vals: jnp.concatenate([jnp.broadcast_to(x, (L, dk)) for x in vals], axis=0)

    mid_full = rows_of(b_mid)
    qe = q * jnp.exp2(b - mid_full)
    ki = k * jnp.exp2(mid_full - b)
    ke_loc = (ki * rows_of([jnp.exp2(b_last[c] - b_mid[c]) for c in chunks])).astype(BF16)
    chunk_rows = lambda a, c: a[c * L:(c + 1) * L, :]
    qx_parts, ke_parts = [], []
    for c_src in order[:-1]:
        ref = scanned[c_src] + b_last[c_src]
        later = [(c > c_src) if forward else (c < c_src) for c in chunks]
        qx_parts.append(jnp.concatenate(
            [(chunk_rows(qe, c) * jnp.exp2(b_mid[c] + scanned[c] - ref)).astype(BF16) if later[c]
             else jnp.zeros((L, dk), BF16) for c in chunks], axis=0))
        ke_parts.append(jnp.concatenate(
            [chunk_rows(ke_loc, c) if c == c_src else jnp.zeros((L, dk), BF16) for c in chunks], axis=0))
    cross = _dot_nt(jnp.concatenate(qx_parts, axis=1), jnp.concatenate(ke_parts, axis=1))
    p = jnp.where(diag_mask, _dot_nt(qe.astype(BF16), ki.astype(BF16)), cross)
    yield None
    vb = v.astype(BF16)
    st = st_ref[...]
    qb = qe * rows_of([jnp.exp2(b_mid[c] + scanned[c]) for c in chunks])
    o = _dot(p.astype(BF16), vb) + _dot_nt(qb.astype(BF16), st.astype(BF16))
    ke = ki * rows_of([jnp.exp2(b_tot - scanned[c] - b_mid[c]) for c in chunks])
    st_ref[...] = st * jnp.exp2(b_tot) + _dot_tn(vb, ke.astype(BF16))
    yield o


def _gla_kernel(qf_ref, kf_ref, vf_ref, zf_ref, qb_ref, kb_ref, vb_ref, zb_ref,
                upf_ref, biasf_ref, upb_ref, biasb_ref, of_ref, ob_ref, sf_ref, sb_ref,
                *, q_scale):
    @pl.when(pl.program_id(2) == 0)
    def _():
        sf_ref[...] = jnp.zeros_like(sf_ref)
        sb_ref[...] = jnp.zeros_like(sb_ref)

    def gate(z_ref, up_ref, bias_ref):
        z = z_ref[...]
        z = jnp.where(lax.broadcasted_iota(I32, z.shape, 1) < 2 * GLA_GATE_RANK, z, 0.0)
        pre = _dot(z.astype(BF16), up_ref[...]) + bias_ref[...]
        return _log_sigmoid(pre) * (math.log2(math.e) / GLA_GATE_NORMALIZER)

    g_f = gate(zf_ref, upf_ref, biasf_ref)
    g_b = gate(zb_ref, upb_ref, biasb_ref)

    dv, dk = sf_ref.shape[1:]
    blocks = []
    for hh in range(GLA_HEADS_PER_STEP):
        ck = slice(hh * dk, (hh + 1) * dk)
        cv = slice(hh * dv, (hh + 1) * dv)
        blocks.append((of_ref, cv, _gla_block(qf_ref[:, ck] * q_scale, kf_ref[:, ck], vf_ref[:, cv],
                                              g_f[:, ck], sf_ref.at[hh], True)))
        blocks.append((ob_ref, cv, _gla_block(qb_ref[:, ck] * q_scale, kb_ref[:, ck], vb_ref[:, cv],
                                              g_b[:, ck], sb_ref.at[hh], False)))
    for _ in range(GLA_STAGES - 1):
        for _, _, blk in blocks:
            next(blk)
    for out_ref, cv, blk in blocks:
        out_ref[:, cv] = next(blk)


def _gla(pm, z, upf, biasf, upb, biasb, batch, seq, dk, dv):
    n = pm.shape[0]
    nb = seq // GLA_BLOCK
    H = GLA_HEADS
    hs = GLA_HEADS_PER_STEP
    kw = H * dk
    fwd = lambda b, h, i: b * nb + i
    bwd = lambda b, h, i: b * nb + nb - 1 - i
    k_col = kw // (hs * dk)
    v_col = 2 * kw // (hs * dv)

    def specs(rowf):
        return [pl.BlockSpec((GLA_BLOCK, hs * dk), lambda b, h, i: (rowf(b, h, i), h)),
                pl.BlockSpec((GLA_BLOCK, hs * dk), lambda b, h, i: (rowf(b, h, i), k_col + h)),
                pl.BlockSpec((GLA_BLOCK, hs * dv), lambda b, h, i: (rowf(b, h, i), v_col + h)),
                pl.BlockSpec((GLA_BLOCK, LANES), lambda b, h, i: (rowf(b, h, i), 0))]

    w_specs = [pl.BlockSpec((LANES, hs * dk), lambda b, h, i: (0, h)),
               pl.BlockSpec((1, hs * dk), lambda b, h, i: (0, h))]
    out_shape = jax.ShapeDtypeStruct((n, H * dv), F32)
    return pl.pallas_call(
        functools.partial(_gla_kernel, q_scale=dk ** -0.5),
        grid=(batch, H // hs, nb),
        in_specs=specs(fwd) + specs(bwd) + w_specs + w_specs,
        out_specs=[pl.BlockSpec((GLA_BLOCK, hs * dv), lambda b, h, i: (fwd(b, h, i), h)),
                   pl.BlockSpec((GLA_BLOCK, hs * dv), lambda b, h, i: (bwd(b, h, i), h))],
        out_shape=[out_shape, out_shape],
        scratch_shapes=[pltpu.VMEM((hs, dv, dk), F32), pltpu.VMEM((hs, dv, dk), F32)],
        compiler_params=_params(("parallel", "parallel", "arbitrary"), 48),
        name="gla",
    )(pm, pm, pm, z, pm, pm, pm, z, upf, biasf, upb, biasb)


ATTN_TQ = 256
ATTN_TK = 256
ATTN_SUB = 64
ATTN_VROWS = 2 * DIFF_HD + 16
ATTN_VT_CHUNK = 512
ATTN_HEADS_PER_STEP = 2


def _diff_attn_kernel(q_ref, k_ref, v_ref, lam_ref, w_ref, o_ref, s_ref, vt_ref, *, lam_init, eps):
    seq = k_ref.shape[0]
    vd = 2 * DIFF_HD
    heads = range(ATTN_HEADS_PER_STEP)
    cols = [slice(h * vd, (h + 1) * vd) for h in heads]
    pad_row = lax.broadcasted_iota(I32, (ATTN_VROWS - vd, seq), 0)
    for h in heads:
        for r in range(0, seq, ATTN_VT_CHUNK):
            vt_ref[h, :vd, r:r + ATTN_VT_CHUNK] = v_ref[r:r + ATTN_VT_CHUNK, cols[h]].astype(F32).T.astype(BF16)
        vt_ref[h, vd:, :] = jnp.where(pad_row == 0, 1.0, 0.0).astype(BF16)

    lv = lam_ref[...]
    lam = (jnp.exp(jnp.sum(lv[0:1] * lv[1:2], axis=-1, keepdims=True))
           - jnp.exp(jnp.sum(lv[2:3] * lv[3:4], axis=-1, keepdims=True)) + lam_init)
    lane = lax.broadcasted_iota(I32, (ATTN_TQ, vd), 1)
    w_col = w_ref[...]

    n_tiles = seq // ATTN_TQ
    n_chunks = seq // ATTN_TK
    groups = ATTN_TK // 8

    def masked_q(h, tile, c):
        q = q_ref[pl.ds(pl.multiple_of(tile * ATTN_TQ, ATTN_TQ), ATTN_TQ), cols[h]]
        keep = (lane < DIFF_HD) if c == 0 else (lane >= DIFF_HD)
        return jnp.where(keep, q, jnp.zeros_like(q))

    def score_chunk(h, qc, c, j, mrun):
        st = _dot_nt(k_ref[j * ATTN_TK:(j + 1) * ATTN_TK, cols[h]], qc)
        s_ref[h, c, j * ATTN_TK:(j + 1) * ATTN_TK, :] = st
        return jnp.maximum(mrun, jnp.max(st.reshape(groups, 8, ATTN_TQ), axis=0))

    def value_chunk(h, c, j, m, acc):
        pieces = [jnp.exp2(s_ref[h, c, r:r + ATTN_SUB, :] - m).astype(BF16)
                  for r in range(j * ATTN_TK, (j + 1) * ATTN_TK, ATTN_SUB)]
        return acc + _dot(vt_ref[h, :, j * ATTN_TK:(j + 1) * ATTN_TK], jnp.concatenate(pieces, axis=0))

    def scores(h, tile, c):
        qc = masked_q(h, tile, c)
        mrun = jnp.full((8, ATTN_TQ), -jnp.inf, F32)
        for j in range(n_chunks):
            mrun = score_chunk(h, qc, c, j, mrun)
        return jnp.max(mrun, axis=0, keepdims=True)

    def overlapped(h, c_val, m, tile, c_score):
        qc = masked_q(h, tile, c_score)
        mrun = jnp.full((8, ATTN_TQ), -jnp.inf, F32)
        acc = jnp.zeros((ATTN_VROWS, ATTN_TQ), F32)
        for j in range(n_chunks):
            acc = value_chunk(h, c_val, j, m, acc)
        for j in range(n_chunks):
            mrun = score_chunk(h, qc, c_score, j, mrun)
        o = acc[:vd] / acc[vd:vd + 1]
        return o, jnp.max(mrun, axis=0, keepdims=True)

    def q_tile(i, m0):
        nxt = jnp.minimum(i + 1, n_tiles - 1)
        first = [overlapped(h, 0, m0[h], i, 1) for h in heads]
        second = [overlapped(h, 1, first[h][1], nxt, 0) for h in heads]
        for h in heads:
            o = first[h][0] - lam * second[h][0]
            ms = jnp.mean(o * o, axis=0, keepdims=True)
            y = (o * lax.rsqrt(ms + eps) * w_col) * (1.0 - lam_init)
            o_ref[pl.ds(pl.multiple_of(i * ATTN_TQ, ATTN_TQ), ATTN_TQ), cols[h]] = y.T
        return tuple(second[h][1] for h in heads)

    lax.fori_loop(0, n_tiles, q_tile, tuple(scores(h, 0, 0) for h in heads))


def _diff_attn(qk, v, lam_vecs, subln_w, batch, seq, heads, lam_init):
    n = qk.shape[0]
    vd = 2 * DIFF_HD
    hs = ATTN_HEADS_PER_STEP
    steps = heads // hs
    return pl.pallas_call(
        functools.partial(_diff_attn_kernel, lam_init=lam_init, eps=SUBLN_EPS),
        grid=(batch, steps),
        in_specs=[pl.BlockSpec((seq, hs * vd), lambda b, h: (b, h)),
                  pl.BlockSpec((seq, hs * vd), lambda b, h: (b, steps + h)),
                  pl.BlockSpec((seq, hs * vd), lambda b, h: (b, h)),
                  pl.BlockSpec((8, LANES), lambda b, h: (0, 0)),
                  pl.BlockSpec((vd, 1), lambda b, h: (0, 0))],
        out_specs=pl.BlockSpec((seq, hs * vd), lambda b, h: (b, h)),
        out_shape=jax.ShapeDtypeStruct((n, heads * vd), F32),
        scratch_shapes=[pltpu.VMEM((hs, 2, seq, ATTN_TQ), F32), pltpu.VMEM((hs, ATTN_VROWS, seq), BF16)],
        compiler_params=_params(("parallel", "parallel"), 48),
        name="diff_attn",
    )(qk, qk, v, lam_vecs, subln_w.reshape(vd, 1))


def _merge_kernel(of_ref, ob_ref, r_ref, yb_ref, ga_ref, gb_ref, x_ref, wout_ref, gnw_ref,
                  fnw_ref, wr_ref, x1_ref, h2_ref, aff_ref, *, dv, n_experts):
    o = of_ref[...] + ob_ref[...]
    segs = []
    for hh in range(o.shape[1] // dv):
        seg = o[:, hh * dv:(hh + 1) * dv]
        ms = jnp.mean(seg * seg, axis=-1, keepdims=True)
        segs.append(seg * lax.rsqrt(ms + NORM_EPS))
    on = jnp.concatenate(segs, axis=1) * gnw_ref[...]
    r = r_ref[...]
    ya = on * (r * jax.nn.sigmoid(r))
    merged = jax.nn.sigmoid(ga_ref[...]) * ya + jax.nn.sigmoid(gb_ref[...]) * yb_ref[...]
    x1 = x_ref[...] + _dot(merged.astype(BF16), wout_ref[...])
    x1_ref[...] = x1
    ms = jnp.mean(x1 * x1, axis=-1, keepdims=True)
    h2 = (x1 * lax.rsqrt(ms + NORM_EPS) * fnw_ref[...]).astype(BF16)
    h2_ref[...] = _pack_halves(h2)
    logits = _dot(h2, wr_ref[...])
    lane = lax.broadcasted_iota(I32, logits.shape, 1)
    logits = jnp.where(lane < n_experts, logits, -jnp.inf)
    e = jnp.exp(logits - jnp.max(logits, axis=-1, keepdims=True))
    aff_ref[...] = e / jnp.sum(e, axis=-1, keepdims=True)


def _merge(o_f, o_b, pm, yb, gates, x, wout, gnw, fnw, wr, dv, tm=256):
    n, d = x.shape
    r_col = (pm.shape[1] - d) // d
    row = lambda i: (i, 0)
    const = lambda i: (0, 0)
    return pl.pallas_call(
        functools.partial(_merge_kernel, dv=dv, n_experts=N_EXPERTS),
        grid=(n // tm,),
        in_specs=[pl.BlockSpec((tm, d), row), pl.BlockSpec((tm, d), row),
                  pl.BlockSpec((tm, d), lambda i: (i, r_col)),
                  pl.BlockSpec((tm, d), row),
                  pl.BlockSpec((tm, d), lambda i: (i, 0)), pl.BlockSpec((tm, d), lambda i: (i, 1)),
                  pl.BlockSpec((tm, d), row),
                  pl.BlockSpec((d, d), const, pipeline_mode=pl.Buffered(1)),
                  pl.BlockSpec((1, d), const),
                  pl.BlockSpec((1, d), const), pl.BlockSpec((d, LANES), const)],
        out_specs=[pl.BlockSpec((tm, d), row), pl.BlockSpec((tm, d // 2), row),
                   pl.BlockSpec((tm, LANES), row)],
        out_shape=[jax.ShapeDtypeStruct((n, d), F32), jax.ShapeDtypeStruct((n, d // 2), U32),
                   jax.ShapeDtypeStruct((n, LANES), F32)],
        compiler_params=_params(("parallel",), 56),
        name="merge_outproj",
    )(o_f, o_b, pm, yb, gates, gates, x, wout, gnw, fnw, wr)


TOPK_LANE_BLOCK = 256


def _exclusive_prefix_count(flags, strict_upper):
    e, t = flags.shape
    carry = jnp.zeros((e, 1), F32)
    parts = []
    for blk in range(t // TOPK_LANE_BLOCK):
        f = flags[:, blk * TOPK_LANE_BLOCK:(blk + 1) * TOPK_LANE_BLOCK]
        parts.append(_dot(f.astype(BF16), strict_upper) + carry)
        carry = carry + jnp.sum(f, axis=-1, keepdims=True)
    return jnp.concatenate(parts, axis=1)


def _topk_kernel(aff_ref, idx_ref, g_ref, posm_ref, *, cap, n_exp):
    av = aff_ref[0]
    t = av.shape[0]
    a = av.T[:n_exp, :]
    bits = pltpu.bitcast(a, I32)

    def search(i, thr):
        cand = thr | jnp.left_shift(jnp.int32(1), 30 - i)
        cnt = jnp.sum(jnp.where(bits >= cand, 1.0, 0.0), axis=-1, keepdims=True)
        return jnp.where(cnt >= cap, cand, thr)

    thr = lax.fori_loop(0, 31, search, jnp.zeros((n_exp, 1), I32))
    gt = jnp.where(bits > thr, 1.0, 0.0)
    eq = jnp.where(bits == thr, 1.0, 0.0)
    need = cap - jnp.sum(gt, axis=-1, keepdims=True)

    r = lax.broadcasted_iota(I32, (TOPK_LANE_BLOCK, TOPK_LANE_BLOCK), 0)
    c = lax.broadcasted_iota(I32, (TOPK_LANE_BLOCK, TOPK_LANE_BLOCK), 1)
    strict_upper = jnp.where(r < c, 1.0, 0.0).astype(BF16)

    tie_rank = _exclusive_prefix_count(eq, strict_upper)
    sel = gt + eq * jnp.where(tie_rank < need, 1.0, 0.0)
    pos = _exclusive_prefix_count(sel, strict_upper)
    posm_ref[...] = jnp.where(sel > 0.5, pos, -1.0)

    hi, mid, lo = _split3(av)
    lane = lax.broadcasted_iota(I32, av.shape, 1)
    tok = lax.broadcasted_iota(I32, av.shape, 0)
    digits = jnp.where(lane == 0, (tok >> 6).astype(F32),
                       jnp.where(lane == 1, (tok & 63).astype(F32), 0.0))
    feat = (digits + pltpu.roll(hi.astype(F32), 2, 1) + pltpu.roll(mid.astype(F32), 2 + n_exp, 1)
            + pltpu.roll(lo.astype(F32), 2 + 2 * n_exp, 1)).astype(BF16)

    slot = lax.broadcasted_iota(I32, (cap, t), 0).astype(F32)
    out_lane = lax.broadcasted_iota(I32, (cap, LANES), 1)

    def per_expert(e, _):
        onehot = jnp.where(posm_ref[pl.ds(e, 1), :] == slot, 1.0, 0.0).astype(BF16)
        res = _dot(onehot, feat)
        idx_ref[0, e] = (res[:, 0:1] * 64.0 + res[:, 1:2]).astype(I32)
        mine = (out_lane == 2 + e) | (out_lane == 2 + n_exp + e) | (out_lane == 2 + 2 * n_exp + e)
        g_ref[0, e] = jnp.sum(jnp.where(mine, res, 0.0), axis=-1, keepdims=True)
        return 0

    lax.fori_loop(0, n_exp, per_expert, 0, unroll=8)


def _topk(aff, cap, n_exp):
    batch, t, _ = aff.shape
    return pl.pallas_call(
        functools.partial(_topk_kernel, cap=cap, n_exp=n_exp),
        grid=(batch,),
        in_specs=[pl.BlockSpec((1, t, LANES), lambda b: (b, 0, 0))],
        out_specs=[pl.BlockSpec((1, n_exp, cap, 1), lambda b: (b, 0, 0, 0)),
                   pl.BlockSpec((1, n_exp, cap, 1), lambda b: (b, 0, 0, 0))],
        out_shape=[jax.ShapeDtypeStruct((batch, n_exp, cap, 1), I32),
                   jax.ShapeDtypeStruct((batch, n_exp, cap, 1), F32)],
        scratch_shapes=[pltpu.VMEM((n_exp, t), F32)],
        compiler_params=_params(("parallel",), 48),
        name="expert_topk",
    )(aff)


def _dispatch_kernel(idx_ref, h_ref, o_ref, *, cap, n_exp):
    base = (pl.program_id(0) * n_exp + pl.program_id(1)) * cap

    def body(c, _):
        o_ref[pl.ds(c, 1), :] = h_ref[pl.ds(idx_ref[base + c], 1), :]
        return 0

    lax.fori_loop(0, cap, body, 0, unroll=8)


def _dispatch(idx_flat, hp, batch, seq, cap):
    width = hp.shape[1]
    return pl.pallas_call(
        functools.partial(_dispatch_kernel, cap=cap, n_exp=N_EXPERTS),
        grid_spec=pltpu.PrefetchScalarGridSpec(
            num_scalar_prefetch=1,
            grid=(batch, N_EXPERTS),
            in_specs=[pl.BlockSpec((seq, width), lambda b, e, idx: (b, 0))],
            out_specs=pl.BlockSpec((None, cap, width), lambda b, e, idx: (e, b, 0))),
        out_shape=jax.ShapeDtypeStruct((N_EXPERTS, batch * cap, width), hp.dtype),
        compiler_params=_params(("arbitrary", "arbitrary"), 48),
        name="dispatch",
    )(idx_flat, hp)


def _ffn_kernel(x_ref, wg_ref, wu_ref, wd_ref, g_ref, o_ref, xs_ref):
    f = pl.program_id(1)
    d = xs_ref.shape[1]

    @pl.when(f == 0)
    def _():
        lo, hi = _unpack_halves(x_ref[...])
        xs_ref[:, :d // 2] = lo
        xs_ref[:, d // 2:] = hi
        o_ref[...] = jnp.zeros_like(o_ref)

    x = xs_ref[...]
    a = _dot(x, wg_ref[...].astype(BF16))
    u = _dot(x, wu_ref[...].astype(BF16))
    hid = (a * jax.nn.sigmoid(a) * u).astype(BF16)
    o_ref[...] += _dot(hid, wd_ref[...].astype(BF16))

    @pl.when(f == pl.num_programs(1) - 1)
    def _():
        o_ref[...] = o_ref[...] * g_ref[...].reshape(o_ref.shape[0], 1)


def _ffn(xe, wg, wu, wd, g, tf=256):
    n_exp, rows, half = xe.shape
    batch, _, cap, _ = g.shape
    d = 2 * half
    dff = wg.shape[2]
    return pl.pallas_call(
        _ffn_kernel,
        grid=(n_exp, dff // tf),
        in_specs=[pl.BlockSpec((None, rows, half), lambda e, f: (e, 0, 0)),
                  pl.BlockSpec((None, d, tf), lambda e, f: (e, 0, f)),
                  pl.BlockSpec((None, d, tf), lambda e, f: (e, 0, f)),
                  pl.BlockSpec((None, tf, d), lambda e, f: (e, f, 0)),
                  pl.BlockSpec((batch, None, cap, 1), lambda e, f: (0, e, 0, 0))],
        out_specs=pl.BlockSpec((None, rows, d), lambda e, f: (e, 0, 0)),
        out_shape=jax.ShapeDtypeStruct((n_exp, rows, d), F32),
        scratch_shapes=[pltpu.VMEM((rows, d), BF16)],
        compiler_params=_params(("parallel", "arbitrary"), 56),
        name="expert_ffn",
    )(xe, wg, wu, wd, g)


COMBINE_UNROLL = 8


def _combine_kernel(idx_ref, x1_ref, ye_ref, w_ref, o_ref, acc_ref, *, cap, n_exp):
    b = pl.program_id(0)
    s = pl.program_id(1)
    chunk = o_ref.shape[0]

    @pl.when(s == 0)
    def _():
        acc_ref[...] = jnp.zeros_like(acc_ref)

    @pl.when(s < n_exp)
    def _():
        rows = pl.ds(pl.multiple_of(s * chunk, chunk), chunk)
        acc_ref[rows, :] = acc_ref[rows, :] + x1_ref[...]
        base = (b * n_exp + s) * cap

        def group(i, _):
            c = i * COMBINE_UNROLL
            dst = [pl.ds(idx_ref[base + c + k], 1) for k in range(COMBINE_UNROLL)]
            sums = [acc_ref[dst[k], :] + ye_ref[pl.ds(c + k, 1), :] for k in range(COMBINE_UNROLL)]
            for k in range(COMBINE_UNROLL):
                acc_ref[dst[k], :] = sums[k]
            return 0

        lax.fori_loop(0, cap // COMBINE_UNROLL, group, 0)

    @pl.when(s >= n_exp)
    def _():
        x = acc_ref[pl.ds(pl.multiple_of((s - n_exp) * chunk, chunk), chunk), :]
        ms = jnp.mean(x * x, axis=-1, keepdims=True)
        o_ref[...] = x * lax.rsqrt(ms + NORM_EPS) * w_ref[...]


def _combine(idx_flat, x1, ye, w, batch, seq, cap):
    n, d = x1.shape
    n_exp = N_EXPERTS
    chunk = seq // n_exp
    assert cap % COMBINE_UNROLL == 0
    return pl.pallas_call(
        functools.partial(_combine_kernel, cap=cap, n_exp=n_exp),
        grid_spec=pltpu.PrefetchScalarGridSpec(
            num_scalar_prefetch=1,
            grid=(batch, 2 * n_exp),
            in_specs=[pl.BlockSpec((chunk, d), lambda b, s, idx: (b * n_exp + jnp.minimum(s, n_exp - 1), 0)),
                      pl.BlockSpec((None, cap, d), lambda b, s, idx: (jnp.minimum(s, n_exp - 1), b, 0)),
                      pl.BlockSpec((1, d), lambda b, s, idx: (0, 0))],
            out_specs=pl.BlockSpec((chunk, d), lambda b, s, idx: (b * n_exp + jnp.maximum(s - n_exp, 0), 0)),
            scratch_shapes=[pltpu.VMEM((seq, d), F32)]),
        out_shape=jax.ShapeDtypeStruct((n, d), F32),
        compiler_params=_params(("arbitrary", "arbitrary"), 56),
        name="combine_norm",
    )(idx_flat, x1, ye, w.reshape(1, d))


def _rope_tables(seq):
    pos = jnp.arange(seq, dtype=F32)
    inv_freq = ROPE_THETA ** (-jnp.arange(0, DIFF_HD, 2, dtype=F32) / DIFF_HD)
    ang = pos[:, None] * inv_freq[None, :]
    cos, sin = jnp.cos(ang), jnp.sin(ang)
    reps = LANES // (DIFF_HD // 2)
    sign = jnp.tile(jnp.concatenate([-jnp.ones((DIFF_HD // 2,), F32), jnp.ones((DIFF_HD // 2,), F32)]),
                    LANES // DIFF_HD)
    return jnp.tile(cos, (1, reps)), jnp.tile(sin, (1, reps)) * sign[None, :]


def kernel(x, norm_mix_w, w_in, gla_gate_up_fwd, gla_gate_bias_fwd, gla_gate_up_bwd, gla_gate_bias_bwd, gla_norm_w, diff_lambda_q1, diff_lambda_k1, diff_lambda_q2, diff_lambda_k2, diff_subln_w, w_out, norm_ffn_w, w_router, w_gate_e, w_up_e, w_down_e, norm_final_w):
    batch, seq, d = x.shape
    depth = w_in.shape[0]
    assert depth == 1, "the combine stage applies the final norm, so it must follow the only layer"
    n = batch * seq
    kw = gla_gate_up_fwd.shape[2]
    dk = kw // GLA_HEADS
    dv = d // GLA_HEADS
    rank = gla_gate_up_fwd.shape[1]
    diff_heads = d // (2 * DIFF_HD)
    cap = CAPACITY_FACTOR * seq // N_EXPERTS
    cos_t, sin_t = _rope_tables(seq)

    o_z = 2 * kw + 2 * d
    o_dq = o_z + 2 * rank
    o_dv = o_dq + 2 * d
    o_gate = o_dv + d

    xf = x.reshape(n, d)
    for l in range(depth):
        w = jnp.transpose(w_in[l])
        h, z = _rmsnorm(xf, norm_mix_w[l], w, o_z, NORM_EPS, BF16)
        pm = _proj(h, w, 0, o_z, F32)
        qk = _proj(h, w, o_dq, 2 * d, BF16,
                   rope=(cos_t, sin_t, d, DIFF_HD ** -0.5 * math.log2(math.e), seq))
        vd = _proj(h, w, o_dv, d, BF16)
        gates = _proj(h, w, o_gate, 2 * d, F32)

        zero_rows = lambda a, before: jnp.pad(a, ((before, LANES - rank - before), (0, 0))).astype(BF16)
        o_f, o_b = _gla(pm, z, zero_rows(gla_gate_up_fwd[l], 0), gla_gate_bias_fwd[l].reshape(1, kw),
                        zero_rows(gla_gate_up_bwd[l], rank), gla_gate_bias_bwd[l].reshape(1, kw),
                        batch, seq, dk, dv)

        lam_init = 0.8 - 0.6 * math.exp(-0.3 * l)
        lam_vecs = jnp.pad(jnp.stack([diff_lambda_q1[l], diff_lambda_k1[l],
                                      diff_lambda_q2[l], diff_lambda_k2[l]]).astype(F32),
                           ((0, 4), (0, LANES - DIFF_HD)))
        yb = _diff_attn(qk, vd, lam_vecs, diff_subln_w[l], batch, seq, diff_heads, lam_init)

        wr = jnp.pad(w_router[l], ((0, 0), (0, LANES - N_EXPERTS))).astype(BF16)
        x1, h2, aff = _merge(o_f, o_b, pm, yb, gates, xf, w_out[l].astype(BF16),
                             jnp.tile(gla_norm_w[l], GLA_HEADS).reshape(1, d),
                             norm_ffn_w[l].reshape(1, d), wr, dv)

        idx, g = _topk(aff.reshape(batch, seq, LANES), cap, N_EXPERTS)
        idx_flat = idx.reshape(batch * N_EXPERTS * cap)

        xe = _dispatch(idx_flat, h2, batch, seq, cap)
        ye = _ffn(xe, w_gate_e[l], w_up_e[l], w_down_e[l], g)

        xf = _combine(idx_flat, x1, ye, norm_final_w, batch, seq, cap)
    return xf.reshape(batch, seq, d)
```

```python
import functools
import math

import jax
import jax.numpy as jnp
from jax import lax
from jax.experimental import pallas as pl
from jax.experimental.pallas import tpu as pltpu

F32 = jnp.float32
BF16 = jnp.bfloat16
I32 = jnp.int32
U32 = jnp.uint32

GLA_HEADS = 4
GLA_GATE_RANK = 16
GLA_GATE_NORMALIZER = 16.0
GLA_CHUNK = 64
DIFF_HD = 64
ROPE_THETA = 10000.0
N_EXPERTS = 16
CAPACITY_FACTOR = 2
NORM_EPS = 1e-6
SUBLN_EPS = 1e-5

LANES = 128
VMEM_PHYSICAL = 64 * 1024 * 1024


def _params(semantics, vmem_mb):
    return pltpu.CompilerParams(dimension_semantics=semantics,
                                vmem_limit_bytes=vmem_mb * 1024 * 1024)


def _dot(a, b):
    return jnp.dot(a, b, preferred_element_type=F32)


def _dot_nt(a, b):
    return lax.dot_general(a, b, (((1,), (1,)), ((), ())), preferred_element_type=F32)


def _dot_tn(a, b):
    return lax.dot_general(a, b, (((0,), (0,)), ((), ())), preferred_element_type=F32)


def _pack_halves(x):
    n = x.shape[1] // 2
    lo = pltpu.bitcast(x[:, :n].astype(F32), U32)
    hi = pltpu.bitcast(x[:, n:].astype(F32), U32)
    return (lo >> 16) | (hi & jnp.uint32(0xFFFF0000))


def _unpack_halves(p):
    lo = pltpu.bitcast(p << 16, F32).astype(BF16)
    hi = pltpu.bitcast(p & jnp.uint32(0xFFFF0000), F32).astype(BF16)
    return lo, hi


def _split3(x):
    hi = x.astype(BF16)
    r1 = x - hi.astype(F32)
    mid = r1.astype(BF16)
    lo = (r1 - mid.astype(F32)).astype(BF16)
    return hi, mid, lo


def _rmsnorm_kernel(x_ref, w_ref, wz_ref, o_ref, z_ref, *, eps):
    x = x_ref[...]
    ms = jnp.mean(x * x, axis=-1, keepdims=True)
    h = (x * lax.rsqrt(ms + eps) * w_ref[...]).astype(o_ref.dtype)
    o_ref[...] = h
    z_ref[...] = _dot_nt(h, wz_ref[...].astype(BF16))


def _rmsnorm(x, w, wt, z_row0, eps, out_dtype, tm=512):
    n, d = x.shape
    assert z_row0 % LANES == 0
    return pl.pallas_call(
        functools.partial(_rmsnorm_kernel, eps=eps),
        grid=(n // tm,),
        in_specs=[pl.BlockSpec((tm, d), lambda i: (i, 0)),
                  pl.BlockSpec((1, d), lambda i: (0, 0)),
                  pl.BlockSpec((LANES, d), lambda i: (z_row0 // LANES, 0))],
        out_specs=[pl.BlockSpec((tm, d), lambda i: (i, 0)),
                   pl.BlockSpec((tm, LANES), lambda i: (i, 0))],
        out_shape=[jax.ShapeDtypeStruct((n, d), out_dtype), jax.ShapeDtypeStruct((n, LANES), F32)],
        compiler_params=_params(("parallel",), 32),
        name="rmsnorm",
    )(x, w.reshape(1, d), wt)


def _proj_kernel(*refs, shift, rope, n_q_tiles, q_scale, act):
    h_ref, wa_ref = refs[0], refs[1]
    pos = 2
    wb_ref = None
    if shift:
        wb_ref = refs[pos]
        pos += 1
    if rope:
        cos_ref, sin_ref = refs[pos], refs[pos + 1]
        pos += 2
    o_ref, w_scr = refs[pos], refs[pos + 1]
    tn = wa_ref.shape[0]

    @pl.when(pl.program_id(1) == 0)
    def _():
        if shift:
            w_scr[:tn - shift, :] = wa_ref[shift:, :].astype(BF16)
            w_scr[tn - shift:, :] = wb_ref[...].astype(BF16)
        else:
            w_scr[...] = wa_ref[...].astype(BF16)

    y = _dot_nt(h_ref[...], w_scr[...])
    if not rope:
        if act is None:
            o_ref[...] = y.astype(o_ref.dtype)
            return
        s = jax.nn.sigmoid(y)
        o_ref[...] = (s if act == "sigmoid" else y * s).astype(o_ref.dtype)
        return
    tm = y.shape[0]
    scale = jnp.where(pl.program_id(0) < n_q_tiles, q_scale, 1.0).astype(F32)
    cos = cos_ref[...]
    sin = sin_ref[...]
    lane = lax.broadcasted_iota(I32, (tm, LANES), 1)
    first_half = (lane & (DIFF_HD - 1)) < (DIFF_HD // 2)
    for c in range(tn // LANES):
        yc = y[:, c * LANES:(c + 1) * LANES]
        sw = jnp.where(first_half,
                       pltpu.roll(yc, LANES - DIFF_HD // 2, 1),
                       pltpu.roll(yc, DIFF_HD // 2, 1))
        o_ref[:, c * LANES:(c + 1) * LANES] = ((yc * cos + sw * sin) * scale).astype(o_ref.dtype)


BF16_SUBLANES = 16


def _proj(h, wt, col0, width, out_dtype, tm=1024, tn=1024, rope=None, act=None):
    n, k = h.shape
    tn = min(tn, width)
    shift = col0 % LANES
    base = col0 - shift
    assert base % tn == 0 and width % tn == 0 and shift % BF16_SUBLANES == 0
    in_specs = [pl.BlockSpec((tm, k), lambda j, i: (i, 0)),
                pl.BlockSpec((tn, k), lambda j, i: (base // tn + j, 0))]
    args = [h, wt]
    if shift:
        assert (base + tn) % shift == 0
        in_specs.append(pl.BlockSpec((shift, k), lambda j, i: ((base + (j + 1) * tn) // shift, 0)))
        args.append(wt)
    n_q_tiles, q_scale = 0, 1.0
    if rope is not None:
        cos, sin, n_q_cols, q_scale, seq = rope
        n_q_tiles = n_q_cols // tn
        blocks_per_seq = seq // tm
        in_specs += [pl.BlockSpec((tm, LANES), lambda j, i: (i % blocks_per_seq, 0)),
                     pl.BlockSpec((tm, LANES), lambda j, i: (i % blocks_per_seq, 0))]
        args += [cos, sin]
    return pl.pallas_call(
        functools.partial(_proj_kernel, shift=shift, rope=rope is not None,
                          n_q_tiles=n_q_tiles, q_scale=q_scale, act=act),
        grid=(width // tn, n // tm),
        in_specs=in_specs,
        out_specs=pl.BlockSpec((tm, tn), lambda j, i: (i, j)),
        out_shape=jax.ShapeDtypeStruct((n, width), out_dtype),
        scratch_shapes=[pltpu.VMEM((tn, k), BF16)],
        compiler_params=_params(("parallel", "arbitrary"), 52),
        name="proj_rope" if rope is not None else "proj",
    )(*args)


GLA_BLOCK = 256
GLA_HEADS_PER_STEP = 4
GLA_STAGES = 3


def _log_sigmoid(x):
    return jnp.minimum(x, 0.0) - jnp.log1p(jnp.exp(-jnp.abs(x)))


def _gla_block(q, k, v, g, st_ref, forward):
    L = GLA_CHUNK
    n = GLA_BLOCK
    n_chunks = n // L
    row = lax.broadcasted_iota(I32, (n, n), 0)
    col = lax.broadcasted_iota(I32, (n, n), 1)
    rowc = row >> (L.bit_length() - 1)
    colc = col >> (L.bit_length() - 1)
    same = rowc == colc
    if forward:
        tri = jnp.where(same & (col <= row), 1.0, 0.0).astype(BF16)
        diag_mask = same & (col <= row)
        order = list(range(n_chunks))
        mid, last = L // 2 - 1, L - 1
    else:
        tri = jnp.where(same & (col >= row), 1.0, 0.0).astype(BF16)
        diag_mask = same & (col > row)
        order = list(range(n_chunks - 1, -1, -1))
        mid, last = L // 2, 0

    g_hi, g_mid, g_lo = _split3(g)
    b = _dot(tri, g_hi) + _dot(tri, g_mid) + _dot(tri, g_lo)
    yield None
    dk = b.shape[1]
    chunks = range(n_chunks)
    b_mid = [b[c * L + mid:c * L + mid + 1, :] for c in chunks]
    b_last = [b[c * L + last:c * L + last + 1, :] for c in chunks]
    scanned = {}
    run = jnp.zeros((1, dk), F32)
    for c in order:
        scanned[c] = run
        run = run + b_last[c]
    b_tot = run
    rows_of = lambda vals: jnp.concatenate([jnp.broadcast_to(x, (L, dk)) for x in vals], axis=0)

    mid_full = rows_of(b_mid)
    qe = q * jnp.exp2(b - mid_full)
    ki = k * jnp.exp2(mid_full - b)
    ke_loc = (ki * rows_of([jnp.exp2(b_last[c] - b_mid[c]) for c in chunks])).astype(BF16)
    chunk_rows = lambda a, c: a[c * L:(c + 1) * L, :]
    qx_parts, ke_parts = [], []
    for c_src in order[:-1]:
        ref = scanned[c_src] + b_last[c_src]
        later = [(c > c_src) if forward else (c < c_src) for c in chunks]
        qx_parts.append(jnp.concatenate(
            [(chunk_rows(qe, c) * jnp.exp2(b_mid[c] + scanned[c] - ref)).astype(BF16) if later[c]
             else jnp.zeros((L, dk), BF16) for c in chunks], axis=0))
        ke_parts.append(jnp.concatenate(
            [chunk_rows(ke_loc, c) if c == c_src else jnp.zeros((L, dk), BF16) for c in chunks], axis=0))
    cross = _dot_nt(jnp.concatenate(qx_parts, axis=1), jnp.concatenate(ke_parts, axis=1))
    p = jnp.where(diag_mask, _dot_nt(qe.astype(BF16), ki.astype(BF16)), cross)
    yield None
    vb = v.astype(BF16)
    st = st_ref[...]
    qb = qe * rows_of([jnp.exp2(b_mid[c] + scanned[c]) for c in chunks])
    o = _dot(p.astype(BF16), vb) + _dot_nt(qb.astype(BF16), st.astype(BF16))
    ke = ki * rows_of([jnp.exp2(b_tot - scanned[c] - b_mid[c]) for c in chunks])
    st_ref[...] = st * jnp.exp2(b_tot) + _dot_tn(vb, ke.astype(BF16))
    yield o


def _gla_kernel(qf_ref, kf_ref, vf_ref, zf_ref, qb_ref, kb_ref, vb_ref, zb_ref,
                upf_ref, biasf_ref, upb_ref, biasb_ref, of_ref, ob_ref, sf_ref, sb_ref,
                *, q_scale):
    @pl.when(pl.program_id(2) == 0)
    def _():
        sf_ref[...] = jnp.zeros_like(sf_ref)
        sb_ref[...] = jnp.zeros_like(sb_ref)

    def gate(z_ref, up_ref, bias_ref):
        z = z_ref[...]
        z = jnp.where(lax.broadcasted_iota(I32, z.shape, 1) < 2 * GLA_GATE_RANK, z, 0.0)
        pre = _dot(z.astype(BF16), up_ref[...]) + bias_ref[...]
        return _log_sigmoid(pre) * (math.log2(math.e) / GLA_GATE_NORMALIZER)

    g_f = gate(zf_ref, upf_ref, biasf_ref)
    g_b = gate(zb_ref, upb_ref, biasb_ref)

    dv, dk = sf_ref.shape[1:]
    blocks = []
    for hh in range(GLA_HEADS_PER_STEP):
        ck = slice(hh * dk, (hh + 1) * dk)
        cv = slice(hh * dv, (hh + 1) * dv)
        blocks.append((of_ref, cv, _gla_block(qf_ref[:, ck] * q_scale, kf_ref[:, ck], vf_ref[:, cv],
                                              g_f[:, ck], sf_ref.at[hh], True)))
        blocks.append((ob_ref, cv, _gla_block(qb_ref[:, ck] * q_scale, kb_ref[:, ck], vb_ref[:, cv],
                                              g_b[:, ck], sb_ref.at[hh], False)))
    for _ in range(GLA_STAGES - 1):
        for _, _, blk in blocks:
            next(blk)
    for out_ref, cv, blk in blocks:
        out_ref[:, cv] = next(blk)


def _gla(pm, z, upf, biasf, upb, biasb, batch, seq, dk, dv):
    n = pm.shape[0]
    nb = seq // GLA_BLOCK
    H = GLA_HEADS
    hs = GLA_HEADS_PER_STEP
    kw = H * dk
    fwd = lambda b, h, i: b * nb + i
    bwd = lambda b, h, i: b * nb + nb - 1 - i
    k_col = kw // (hs * dk)
    v_col = 2 * kw // (hs * dv)

    def specs(rowf):
        return [pl.BlockSpec((GLA_BLOCK, hs * dk), lambda b, h, i: (rowf(b, h, i), h)),
                pl.BlockSpec((GLA_BLOCK, hs * dk), lambda b, h, i: (rowf(b, h, i), k_col + h)),
                pl.BlockSpec((GLA_BLOCK, hs * dv), lambda b, h, i: (rowf(b, h, i), v_col + h)),
                pl.BlockSpec((GLA_BLOCK, LANES), lambda b, h, i: (rowf(b, h, i), 0))]

    w_specs = [pl.BlockSpec((LANES, hs * dk), lambda b, h, i: (0, h)),
               pl.BlockSpec((1, hs * dk), lambda b, h, i: (0, h))]
    out_shape = jax.ShapeDtypeStruct((n, H * dv), F32)
    return pl.pallas_call(
        functools.partial(_gla_kernel, q_scale=dk ** -0.5),
        grid=(batch, H // hs, nb),
        in_specs=specs(fwd) + specs(bwd) + w_specs + w_specs,
        out_specs=[pl.BlockSpec((GLA_BLOCK, hs * dv), lambda b, h, i: (fwd(b, h, i), h)),
                   pl.BlockSpec((GLA_BLOCK, hs * dv), lambda b, h, i: (bwd(b, h, i), h))],
        out_shape=[out_shape, out_shape],
        scratch_shapes=[pltpu.VMEM((hs, dv, dk), F32), pltpu.VMEM((hs, dv, dk), F32)],
        compiler_params=_params(("parallel", "parallel", "arbitrary"), 48),
        name="gla",
    )(pm, pm, pm, z, pm, pm, pm, z, upf, biasf, upb, biasb)


ATTN_TQ = 256
ATTN_TK = 256
ATTN_SUB = 64
ATTN_VROWS = 2 * DIFF_HD + 16
ATTN_VT_CHUNK = 512
ATTN_HEADS_PER_STEP = 2


def _diff_attn_kernel(q_ref, k_ref, v_ref, lam_ref, w_ref, o_ref, s_ref, vt_ref, *, lam_init, eps):
    seq = k_ref.shape[0]
    vd = 2 * DIFF_HD
    heads = range(ATTN_HEADS_PER_STEP)
    cols = [slice(h * vd, (h + 1) * vd) for h in heads]
    pad_row = lax.broadcasted_iota(I32, (ATTN_VROWS - vd, seq), 0)
    for h in heads:
        for r in range(0, seq, ATTN_VT_CHUNK):
            vt_ref[h, :vd, r:r + ATTN_VT_CHUNK] = v_ref[r:r + ATTN_VT_CHUNK, cols[h]].astype(F32).T.astype(BF16)
        vt_ref[h, vd:, :] = jnp.where(pad_row == 0, 1.0, 0.0).astype(BF16)

    lv = lam_ref[...]
    lam = (jnp.exp(jnp.sum(lv[0:1] * lv[1:2], axis=-1, keepdims=True))
           - jnp.exp(jnp.sum(lv[2:3] * lv[3:4], axis=-1, keepdims=True)) + lam_init)
    lane = lax.broadcasted_iota(I32, (ATTN_TQ, vd), 1)
    w_col = w_ref[...]

    n_tiles = seq // ATTN_TQ
    n_chunks = seq // ATTN_TK
    groups = ATTN_TK // 8

    def masked_q(h, tile, c):
        q = q_ref[pl.ds(pl.multiple_of(tile * ATTN_TQ, ATTN_TQ), ATTN_TQ), cols[h]]
        keep = (lane < DIFF_HD) if c == 0 else (lane >= DIFF_HD)
        return jnp.where(keep, q, jnp.zeros_like(q))

    def score_chunk(h, qc, c, j, mrun):
        st = _dot_nt(k_ref[j * ATTN_TK:(j + 1) * ATTN_TK, cols[h]], qc)
        s_ref[h, c, j * ATTN_TK:(j + 1) * ATTN_TK, :] = st
        return jnp.maximum(mrun, jnp.max(st.reshape(groups, 8, ATTN_TQ), axis=0))

    def value_chunk(h, c, j, m, acc):
        pieces = [jnp.exp2(s_ref[h, c, r:r + ATTN_SUB, :] - m).astype(BF16)
                  for r in range(j * ATTN_TK, (j + 1) * ATTN_TK, ATTN_SUB)]
        return acc + _dot(vt_ref[h, :, j * ATTN_TK:(j + 1) * ATTN_TK], jnp.concatenate(pieces, axis=0))

    def scores(h, tile, c):
        qc = masked_q(h, tile, c)
        mrun = jnp.full((8, ATTN_TQ), -jnp.inf, F32)
        for j in range(n_chunks):
            mrun = score_chunk(h, qc, c, j, mrun)
        return jnp.max(mrun, axis=0, keepdims=True)

    def overlapped(h, c_val, m, tile, c_score):
        qc = masked_q(h, tile, c_score)
        mrun = jnp.full((8, ATTN_TQ), -jnp.inf, F32)
        acc = jnp.zeros((ATTN_VROWS, ATTN_TQ), F32)
        for j in range(n_chunks):
            acc = value_chunk(h, c_val, j, m, acc)
        for j in range(n_chunks):
            mrun = score_chunk(h, qc, c_score, j, mrun)
        o = acc[:vd] / acc[vd:vd + 1]
        return o, jnp.max(mrun, axis=0, keepdims=True)

    def q_tile(i, m0):
        nxt = jnp.minimum(i + 1, n_tiles - 1)
        first = [overlapped(h, 0, m0[h], i, 1) for h in heads]
        second = [overlapped(h, 1, first[h][1], nxt, 0) for h in heads]
        for h in heads:
            o = first[h][0] - lam * second[h][0]
            ms = jnp.mean(o * o, axis=0, keepdims=True)
            y = (o * lax.rsqrt(ms + eps) * w_col) * (1.0 - lam_init)
            o_ref[pl.ds(pl.multiple_of(i * ATTN_TQ, ATTN_TQ), ATTN_TQ), cols[h]] = y.T
        return tuple(second[h][1] for h in heads)

    lax.fori_loop(0, n_tiles, q_tile, tuple(scores(h, 0, 0) for h in heads))


def _diff_attn(qk, v, lam_vecs, subln_w, batch, seq, heads, lam_init):
    n = qk.shape[0]
    vd = 2 * DIFF_HD
    hs = ATTN_HEADS_PER_STEP
    steps = heads // hs
    return pl.pallas_call(
        functools.partial(_diff_attn_kernel, lam_init=lam_init, eps=SUBLN_EPS),
        grid=(batch, steps),
        in_specs=[pl.BlockSpec((seq, hs * vd), lambda b, h: (b, h)),
                  pl.BlockSpec((seq, hs * vd), lambda b, h: (b, steps + h)),
                  pl.BlockSpec((seq, hs * vd), lambda b, h: (b, h)),
                  pl.BlockSpec((8, LANES), lambda b, h: (0, 0)),
                  pl.BlockSpec((vd, 1), lambda b, h: (0, 0))],
        out_specs=pl.BlockSpec((seq, hs * vd), lambda b, h: (b, h)),
        out_shape=jax.ShapeDtypeStruct((n, heads * vd), F32),
        scratch_shapes=[pltpu.VMEM((hs, 2, seq, ATTN_TQ), F32), pltpu.VMEM((hs, ATTN_VROWS, seq), BF16)],
        compiler_params=_params(("parallel", "parallel"), 48),
        name="diff_attn",
    )(qk, qk, v, lam_vecs, subln_w.reshape(vd, 1))


def _merge_kernel(of_ref, ob_ref, r_ref, yb_ref, ga_ref, gb_ref, x_ref, wout_ref, gnw_ref,
                  fnw_ref, wr_ref, x1_ref, h2_ref, aff_ref, *, dv, n_experts):
    o = of_ref[...] + ob_ref[...]
    segs = []
    for hh in range(o.shape[1] // dv):
        seg = o[:, hh * dv:(hh + 1) * dv]
        ms = jnp.mean(seg * seg, axis=-1, keepdims=True)
        segs.append(seg * lax.rsqrt(ms + NORM_EPS))
    on = jnp.concatenate(segs, axis=1) * gnw_ref[...]
    r = r_ref[...]
    ya = on * r
    merged = ga_ref[...] * ya + gb_ref[...] * yb_ref[...]
    x1 = x_ref[...] + _dot(merged.astype(BF16), wout_ref[...])
    x1_ref[...] = x1
    ms = jnp.mean(x1 * x1, axis=-1, keepdims=True)
    h2 = (x1 * lax.rsqrt(ms + NORM_EPS) * fnw_ref[...]).astype(BF16)
    h2_ref[...] = _pack_halves(h2)
    logits = _dot(h2, wr_ref[...])
    lane = lax.broadcasted_iota(I32, logits.shape, 1)
    logits = jnp.where(lane < n_experts, logits, -jnp.inf)
    e = jnp.exp(logits - jnp.max(logits, axis=-1, keepdims=True))
    aff_ref[...] = e / jnp.sum(e, axis=-1, keepdims=True)


def _merge(o_f, o_b, r_act, yb, gates, x, wout, gnw, fnw, wr, dv, tm=256):
    n, d = x.shape
    row = lambda i: (i, 0)
    const = lambda i: (0, 0)
    return pl.pallas_call(
        functools.partial(_merge_kernel, dv=dv, n_experts=N_EXPERTS),
        grid=(n // tm,),
        in_specs=[pl.BlockSpec((tm, d), row), pl.BlockSpec((tm, d), row),
                  pl.BlockSpec((tm, d), row),
                  pl.BlockSpec((tm, d), row),
                  pl.BlockSpec((tm, d), lambda i: (i, 0)), pl.BlockSpec((tm, d), lambda i: (i, 1)),
                  pl.BlockSpec((tm, d), row),
                  pl.BlockSpec((d, d), const, pipeline_mode=pl.Buffered(1)),
                  pl.BlockSpec((1, d), const),
                  pl.BlockSpec((1, d), const), pl.BlockSpec((d, LANES), const)],
        out_specs=[pl.BlockSpec((tm, d), row), pl.BlockSpec((tm, d // 2), row),
                   pl.BlockSpec((tm, LANES), row)],
        out_shape=[jax.ShapeDtypeStruct((n, d), F32), jax.ShapeDtypeStruct((n, d // 2), U32),
                   jax.ShapeDtypeStruct((n, LANES), F32)],
        compiler_params=_params(("parallel",), 56),
        name="merge_outproj",
    )(o_f, o_b, r_act, yb, gates, gates, x, wout, gnw, fnw, wr)


TOPK_LANE_BLOCK = 256


def _exclusive_prefix_count(flags, strict_upper):
    e, t = flags.shape
    carry = jnp.zeros((e, 1), F32)
    parts = []
    for blk in range(t // TOPK_LANE_BLOCK):
        f = flags[:, blk * TOPK_LANE_BLOCK:(blk + 1) * TOPK_LANE_BLOCK]
        parts.append(_dot(f.astype(BF16), strict_upper) + carry)
        carry = carry + jnp.sum(f, axis=-1, keepdims=True)
    return jnp.concatenate(parts, axis=1)


def _topk_kernel(aff_ref, idx_ref, g_ref, posm_ref, *, cap, n_exp):
    av = aff_ref[0]
    t = av.shape[0]
    a = av.T[:n_exp, :]
    bits = pltpu.bitcast(a, I32)

    def search(i, thr):
        cand = thr | jnp.left_shift(jnp.int32(1), 30 - i)
        cnt = jnp.sum(jnp.where(bits >= cand, 1.0, 0.0), axis=-1, keepdims=True)
        return jnp.where(cnt >= cap, cand, thr)

    thr = lax.fori_loop(0, 31, search, jnp.zeros((n_exp, 1), I32))
    gt = jnp.where(bits > thr, 1.0, 0.0)
    eq = jnp.where(bits == thr, 1.0, 0.0)
    need = cap - jnp.sum(gt, axis=-1, keepdims=True)

    r = lax.broadcasted_iota(I32, (TOPK_LANE_BLOCK, TOPK_LANE_BLOCK), 0)
    c = lax.broadcasted_iota(I32, (TOPK_LANE_BLOCK, TOPK_LANE_BLOCK), 1)
    strict_upper = jnp.where(r < c, 1.0, 0.0).astype(BF16)

    tie_rank = _exclusive_prefix_count(eq, strict_upper)
    sel = gt + eq * jnp.where(tie_rank < need, 1.0, 0.0)
    pos = _exclusive_prefix_count(sel, strict_upper)
    posm_ref[...] = jnp.where(sel > 0.5, pos, -1.0)

    hi, mid, lo = _split3(av)
    lane = lax.broadcasted_iota(I32, av.shape, 1)
    tok = lax.broadcasted_iota(I32, av.shape, 0)
    digits = jnp.where(lane == 0, (tok >> 6).astype(F32),
                       jnp.where(lane == 1, (tok & 63).astype(F32), 0.0))
    feat = (digits + pltpu.roll(hi.astype(F32), 2, 1) + pltpu.roll(mid.astype(F32), 2 + n_exp, 1)
            + pltpu.roll(lo.astype(F32), 2 + 2 * n_exp, 1)).astype(BF16)

    slot = lax.broadcasted_iota(I32, (cap, t), 0).astype(F32)
    out_lane = lax.broadcasted_iota(I32, (cap, LANES), 1)

    def per_expert(e, _):
        onehot = jnp.where(posm_ref[pl.ds(e, 1), :] == slot, 1.0, 0.0).astype(BF16)
        res = _dot(onehot, feat)
        idx_ref[0, e] = (res[:, 0:1] * 64.0 + res[:, 1:2]).astype(I32)
        mine = (out_lane == 2 + e) | (out_lane == 2 + n_exp + e) | (out_lane == 2 + 2 * n_exp + e)
        g_ref[0, e] = jnp.sum(jnp.where(mine, res, 0.0), axis=-1, keepdims=True)
        return 0

    lax.fori_loop(0, n_exp, per_expert, 0, unroll=8)


def _topk(aff, cap, n_exp):
    batch, t, _ = aff.shape
    return pl.pallas_call(
        functools.partial(_topk_kernel, cap=cap, n_exp=n_exp),
        grid=(batch,),
        in_specs=[pl.BlockSpec((1, t, LANES), lambda b: (b, 0, 0))],
        out_specs=[pl.BlockSpec((1, n_exp, cap, 1), lambda b: (b, 0, 0, 0)),
                   pl.BlockSpec((1, n_exp, cap, 1), lambda b: (b, 0, 0, 0))],
        out_shape=[jax.ShapeDtypeStruct((batch, n_exp, cap, 1), I32),
                   jax.ShapeDtypeStruct((batch, n_exp, cap, 1), F32)],
        scratch_shapes=[pltpu.VMEM((n_exp, t), F32)],
        compiler_params=_params(("parallel",), 48),
        name="expert_topk",
    )(aff)


def _dispatch_kernel(idx_ref, h_ref, o_ref, *, cap, n_exp):
    base = (pl.program_id(0) * n_exp + pl.program_id(1)) * cap

    def body(c, _):
        o_ref[pl.ds(c, 1), :] = h_ref[pl.ds(idx_ref[base + c], 1), :]
        return 0

    lax.fori_loop(0, cap, body, 0, unroll=8)


def _dispatch(idx_flat, hp, batch, seq, cap):
    width = hp.shape[1]
    return pl.pallas_call(
        functools.partial(_dispatch_kernel, cap=cap, n_exp=N_EXPERTS),
        grid_spec=pltpu.PrefetchScalarGridSpec(
            num_scalar_prefetch=1,
            grid=(batch, N_EXPERTS),
            in_specs=[pl.BlockSpec((seq, width), lambda b, e, idx: (b, 0))],
            out_specs=pl.BlockSpec((None, cap, width), lambda b, e, idx: (e, b, 0))),
        out_shape=jax.ShapeDtypeStruct((N_EXPERTS, batch * cap, width), hp.dtype),
        compiler_params=_params(("arbitrary", "arbitrary"), 48),
        name="dispatch",
    )(idx_flat, hp)


def _ffn_kernel(x_ref, wg_ref, wu_ref, wd_ref, g_ref, o_ref, xs_ref):
    f = pl.program_id(1)
    d = xs_ref.shape[1]

    @pl.when(f == 0)
    def _():
        lo, hi = _unpack_halves(x_ref[...])
        xs_ref[:, :d // 2] = lo
        xs_ref[:, d // 2:] = hi
        o_ref[...] = jnp.zeros_like(o_ref)

    x = xs_ref[...]
    a = _dot(x, wg_ref[...].astype(BF16))
    u = _dot(x, wu_ref[...].astype(BF16))
    hid = (a * jax.nn.sigmoid(a) * u).astype(BF16)
    o_ref[...] += _dot(hid, wd_ref[...].astype(BF16))

    @pl.when(f == pl.num_programs(1) - 1)
    def _():
        o_ref[...] = o_ref[...] * g_ref[...].reshape(o_ref.shape[0], 1)


def _ffn(xe, wg, wu, wd, g, tf=256):
    n_exp, rows, half = xe.shape
    batch, _, cap, _ = g.shape
    d = 2 * half
    dff = wg.shape[2]
    return pl.pallas_call(
        _ffn_kernel,
        grid=(n_exp, dff // tf),
        in_specs=[pl.BlockSpec((None, rows, half), lambda e, f: (e, 0, 0)),
                  pl.BlockSpec((None, d, tf), lambda e, f: (e, 0, f)),
                  pl.BlockSpec((None, d, tf), lambda e, f: (e, 0, f)),
                  pl.BlockSpec((None, tf, d), lambda e, f: (e, f, 0)),
                  pl.BlockSpec((batch, None, cap, 1), lambda e, f: (0, e, 0, 0))],
        out_specs=pl.BlockSpec((None, rows, d), lambda e, f: (e, 0, 0)),
        out_shape=jax.ShapeDtypeStruct((n_exp, rows, d), F32),
        scratch_shapes=[pltpu.VMEM((rows, d), BF16)],
        compiler_params=_params(("parallel", "arbitrary"), 56),
        name="expert_ffn",
    )(xe, wg, wu, wd, g)


COMBINE_UNROLL = 8


def _combine_kernel(idx_ref, x1_ref, ye_ref, w_ref, o_ref, acc_ref, *, cap, n_exp):
    b = pl.program_id(0)
    s = pl.program_id(1)
    chunk = o_ref.shape[0]

    @pl.when(s == 0)
    def _():
        acc_ref[...] = jnp.zeros_like(acc_ref)

    @pl.when(s < n_exp)
    def _():
        rows = pl.ds(pl.multiple_of(s * chunk, chunk), chunk)
        acc_ref[rows, :] = acc_ref[rows, :] + x1_ref[...]
        base = (b * n_exp + s) * cap

        def group(i, _):
            c = i * COMBINE_UNROLL
            dst = [pl.ds(idx_ref[base + c + k], 1) for k in range(COMBINE_UNROLL)]
            sums = [acc_ref[dst[k], :] + ye_ref[pl.ds(c + k, 1), :] for k in range(COMBINE_UNROLL)]
            for k in range(COMBINE_UNROLL):
                acc_ref[dst[k], :] = sums[k]
            return 0

        lax.fori_loop(0, cap // COMBINE_UNROLL, group, 0)

    @pl.when(s >= n_exp)
    def _():
        x = acc_ref[pl.ds(pl.multiple_of((s - n_exp) * chunk, chunk), chunk), :]
        ms = jnp.mean(x * x, axis=-1, keepdims=True)
        o_ref[...] = x * lax.rsqrt(ms + NORM_EPS) * w_ref[...]


def _combine(idx_flat, x1, ye, w, batch, seq, cap):
    n, d = x1.shape
    n_exp = N_EXPERTS
    chunk = seq // n_exp
    assert cap % COMBINE_UNROLL == 0
    return pl.pallas_call(
        functools.partial(_combine_kernel, cap=cap, n_exp=n_exp),
        grid_spec=pltpu.PrefetchScalarGridSpec(
            num_scalar_prefetch=1,
            grid=(batch, 2 * n_exp),
            in_specs=[pl.BlockSpec((chunk, d), lambda b, s, idx: (b * n_exp + jnp.minimum(s, n_exp - 1), 0)),
                      pl.BlockSpec((None, cap, d), lambda b, s, idx: (jnp.minimum(s, n_exp - 1), b, 0)),
                      pl.BlockSpec((1, d), lambda b, s, idx: (0, 0))],
            out_specs=pl.BlockSpec((chunk, d), lambda b, s, idx: (b * n_exp + jnp.maximum(s - n_exp, 0), 0)),
            scratch_shapes=[pltpu.VMEM((seq, d), F32)]),
        out_shape=jax.ShapeDtypeStruct((n, d), F32),
        compiler_params=_params(("arbitrary", "arbitrary"), 56),
        name="combine_norm",
    )(idx_flat, x1, ye, w.reshape(1, d))


def _rope_tables(seq):
    pos = jnp.arange(seq, dtype=F32)
    inv_freq = ROPE_THETA ** (-jnp.arange(0, DIFF_HD, 2, dtype=F32) / DIFF_HD)
    ang = pos[:, None] * inv_freq[None, :]
    cos, sin = jnp.cos(ang), jnp.sin(ang)
    reps = LANES // (DIFF_HD // 2)
    sign = jnp.tile(jnp.concatenate([-jnp.ones((DIFF_HD // 2,), F32), jnp.ones((DIFF_HD // 2,), F32)]),
                    LANES // DIFF_HD)
    return jnp.tile(cos, (1, reps)), jnp.tile(sin, (1, reps)) * sign[None, :]


def kernel(x, norm_mix_w, w_in, gla_gate_up_fwd, gla_gate_bias_fwd, gla_gate_up_bwd, gla_gate_bias_bwd, gla_norm_w, diff_lambda_q1, diff_lambda_k1, diff_lambda_q2, diff_lambda_k2, diff_subln_w, w_out, norm_ffn_w, w_router, w_gate_e, w_up_e, w_down_e, norm_final_w):
    batch, seq, d = x.shape
    depth = w_in.shape[0]
    assert depth == 1, "the combine stage applies the final norm, so it must follow the only layer"
    n = batch * seq
    kw = gla_gate_up_fwd.shape[2]
    dk = kw // GLA_HEADS
    dv = d // GLA_HEADS
    rank = gla_gate_up_fwd.shape[1]
    diff_heads = d // (2 * DIFF_HD)
    cap = CAPACITY_FACTOR * seq // N_EXPERTS
    cos_t, sin_t = _rope_tables(seq)

    o_z = 2 * kw + 2 * d
    o_dq = o_z + 2 * rank
    o_dv = o_dq + 2 * d
    o_gate = o_dv + d

    xf = x.reshape(n, d)
    for l in range(depth):
        w = jnp.transpose(w_in[l])
        h, z = _rmsnorm(xf, norm_mix_w[l], w, o_z, NORM_EPS, BF16)
        pm = _proj(h, w, 0, 2 * kw + d, F32)
        r_act = _proj(h, w, 2 * kw + d, d, F32, act="silu")
        qk = _proj(h, w, o_dq, 2 * d, BF16,
                   rope=(cos_t, sin_t, d, DIFF_HD ** -0.5 * math.log2(math.e), seq))
        vd = _proj(h, w, o_dv, d, BF16)
        gates = _proj(h, w, o_gate, 2 * d, F32, act="sigmoid")

        zero_rows = lambda a, before: jnp.pad(a, ((before, LANES - rank - before), (0, 0))).astype(BF16)
        o_f, o_b = _gla(pm, z, zero_rows(gla_gate_up_fwd[l], 0), gla_gate_bias_fwd[l].reshape(1, kw),
                        zero_rows(gla_gate_up_bwd[l], rank), gla_gate_bias_bwd[l].reshape(1, kw),
                        batch, seq, dk, dv)

        lam_init = 0.8 - 0.6 * math.exp(-0.3 * l)
        lam_vecs = jnp.pad(jnp.stack([diff_lambda_q1[l], diff_lambda_k1[l],
                                      diff_lambda_q2[l], diff_lambda_k2[l]]).astype(F32),
                           ((0, 4), (0, LANES - DIFF_HD)))
        yb = _diff_attn(qk, vd, lam_vecs, diff_subln_w[l], batch, seq, diff_heads, lam_init)

        wr = jnp.pad(w_router[l], ((0, 0), (0, LANES - N_EXPERTS))).astype(BF16)
        x1, h2, aff = _merge(o_f, o_b, r_act, yb, gates, xf, w_out[l].astype(BF16),
                             jnp.tile(gla_norm_w[l], GLA_HEADS).reshape(1, d),
                             norm_ffn_w[l].reshape(1, d), wr, dv)

        idx, g = _topk(aff.reshape(batch, seq, LANES), cap, N_EXPERTS)
        idx_flat = idx.reshape(batch * N_EXPERTS * cap)

        xe = _dispatch(idx_flat, h2, batch, seq, cap)
        ye = _ffn(xe, w_gate_e[l], w_up_e[l], w_down_e[l], g)

        xf = _combine(idx_flat, x1, ye, norm_final_w, batch, seq, cap)
    return xf.reshape(batch, seq, d)
```
